```python
import math
import jax
import jax.numpy as jnp
from jax import lax
import numpy as np

D_MODEL = 1024
BATCH = 4
SEQ = 4096
DEPTH = 2
DEC_BATCH = 32
DEC_SEQ = 1
PAST_LEN = 8192
PAGE_SIZE = 128

W_A = D_MODEL // 2
G_A = 4
CG_A = W_A // G_A
CHUNK = 128
N_HEADS_B = 4
HEAD_DIM_B = D_MODEL // 16
W_B = N_HEADS_B * 2 * HEAD_DIM_B
Q_BLOCK = 128
W_C = D_MODEL // 2
POOL_WINDOWS = (2, 4, 8, 16)
G_C = len(POOL_WINDOWS)
CG_C = W_C // G_C
MAX_WIN = max(POOL_WINDOWS)
N_BRANCH = 3
N_IN = 2 * W_A + 3 * W_B + W_C + N_BRANCH * D_MODEL
SPLITS = (2 * W_A, 2 * W_A + W_B, 2 * W_A + 2 * W_B, 2 * W_A + 3 * W_B, 2 * W_A + 3 * W_B + W_C)
D_FF = ((8 * D_MODEL // 3 + 127) // 128) * 128
CONV_W = 3
EPS = 1e-6
NEG_INF = -1e30

kernel_name = 'hybrid_gmlp_diffattn_pool_convffn_step'


def rmsnorm(x, g):
    xf = x.astype(jnp.float32)
    xf = xf * lax.rsqrt(jnp.mean(xf * xf, axis=-1, keepdims=True) + EPS)
    return xf.astype(x.dtype) * g


def chunk_spatial_gating(u, vn, ws, bs):
    b, L, _ = u.shape
    nc = -(-L // CHUNK)
    pad = nc * CHUNK - L
    vp = jnp.pad(vn, ((0, 0), (0, pad), (0, 0))).reshape(b, nc, CHUNK, G_A, CG_A)
    causal = jnp.tril(jnp.ones((CHUNK, CHUNK), dtype=bool))
    wm = jnp.where(causal[None], ws, jnp.zeros((), ws.dtype))
    s = jnp.einsum('gts,bnsgc->bntgc', wm, vp) + bs.T[None, None, :, :, None]
    s = s.reshape(b, nc * CHUNK, W_A)[:, :L]
    return u * s


def diff_attention(q, k, v, q_offset, lam):
    b, Lq = q.shape[0], q.shape[1]
    Lk = k.shape[1]
    blk = min(Q_BLOCK, Lq)
    nb = -(-Lq // blk)
    pad = nb * blk - Lq
    qp = jnp.pad(q, ((0, 0), (0, pad), (0, 0), (0, 0), (0, 0)))
    qb = jnp.moveaxis(qp.reshape(b, nb, blk, N_HEADS_B, 2, HEAD_DIM_B), 1, 0)
    k_pos = jnp.arange(Lk)
    scale = HEAD_DIM_B ** -0.5

    def one_block(args):
        qi, bi = args
        s = jnp.einsum('bqhcd,bkhcd->bhcqk', qi, k).astype(jnp.float32) * scale
        q_pos = q_offset + bi * blk + jnp.arange(blk)
        mask = k_pos[None, :] <= q_pos[:, None]
        s = jnp.where(mask, s, NEG_INF)
        pr = jax.nn.softmax(s, axis=-1)
        w = pr[:, :, 0] - lam * pr[:, :, 1]
        return jnp.einsum('bhqk,bkhe->bqhe', w.astype(v.dtype), v)

    out = lax.map(one_block, (qb, jnp.arange(nb)))
    out = jnp.moveaxis(out, 0, 1).reshape(b, nb * blk, N_HEADS_B, 2 * HEAD_DIM_B)
    return out[:, :Lq]


def multiscale_pool(c_ext, start_pos):
    P = MAX_WIN - 1
    L = c_ext.shape[1] - P
    cs = jnp.cumsum(c_ext.astype(jnp.float32), axis=1)
    cs = jnp.pad(cs, ((0, 0), (1, 0), (0, 0)))
    pos = start_pos + jnp.arange(L)
    means = []
    for g, win in enumerate(POOL_WINDOWS):
        sl = slice(g * CG_C, (g + 1) * CG_C)
        wsum = cs[:, P + 1:P + 1 + L, sl] - cs[:, P + 1 - win:P + 1 - win + L, sl]
        cnt = jnp.minimum(pos + 1, win).astype(jnp.float32)
        means.append(wsum / cnt[None, :, None])
    mean = jnp.concatenate(means, axis=-1).astype(c_ext.dtype)
    return mean - c_ext[:, P:]


def conv_ffn(h, prefix, f_up, f_conv_w, f_conv_b, f_down):
    up = h @ f_up
    L = up.shape[1]
    ext = jnp.concatenate([prefix.astype(up.dtype), up], axis=1)
    conv = f_conv_b + f_conv_w[0] * ext[:, 0:L]
    for j in range(1, CONV_W):
        conv = conv + f_conv_w[j] * ext[:, j:j + L]
    gate, val = jnp.split(conv, 2, axis=-1)
    return (jax.nn.silu(gate) * val) @ f_down, ext[:, L:]


def hybrid_layer(x, p, lam_init, k_past, v_past, pool_prefix, conv_prefix, start_pos):
    b, L, _ = x.shape
    h = rmsnorm(x, p['norm1_g'])
    z = h @ p['w_in']
    za, zq, zk, zv, zc, zg = jnp.split(z, SPLITS, axis=-1)
    u, va = jnp.split(jax.nn.gelu(za), 2, axis=-1)
    va = rmsnorm(va, p['a_vnorm_g'])
    out_a = chunk_spatial_gating(u, va, p['a_ws'], p['a_bs'])
    q = rmsnorm(zq.reshape(b, L, N_HEADS_B, 2, HEAD_DIM_B), p['b_qnorm_g'])
    k = rmsnorm(zk.reshape(b, L, N_HEADS_B, 2, HEAD_DIM_B), p['b_knorm_g']).reshape(b, L, N_HEADS_B, 2 * HEAD_DIM_B)
    v = zv.reshape(b, L, N_HEADS_B, 2 * HEAD_DIM_B)
    if k_past is None:
        k_all, v_all = k, v
    else:
        k_all = jnp.concatenate([k_past.astype(k.dtype), k], axis=1)
        v_all = jnp.concatenate([v_past.astype(v.dtype), v], axis=1)
    f32 = jnp.float32
    lam = (jnp.exp(jnp.sum(p['b_lq1'].astype(f32) * p['b_lk1'].astype(f32)))
           - jnp.exp(jnp.sum(p['b_lq2'].astype(f32) * p['b_lk2'].astype(f32))) + lam_init)
    o = diff_attention(q, k_all.reshape(b, -1, N_HEADS_B, 2, HEAD_DIM_B), v_all, start_pos, lam)
    out_b = (rmsnorm(o, p['b_subln_g']) * (1.0 - lam_init)).reshape(b, L, W_B)
    c_ext = jnp.concatenate([pool_prefix.astype(zc.dtype), zc], axis=1)
    pooled = multiscale_pool(c_ext, start_pos)
    out_c = jnp.einsum('blgc,gcd->blgd', pooled.reshape(b, L, G_C, CG_C), p['c_w']).reshape(b, L, W_C) * p['c_scale']
    g_a, g_b, g_c = jnp.split(jax.nn.sigmoid(zg), N_BRANCH, axis=-1)
    merged = g_a * (out_a @ p['p_a']) + g_b * (out_b @ p['p_b']) + g_c * (out_c @ p['p_c'])
    x = x + merged @ p['w_o']
    f, conv_tail = conv_ffn(rmsnorm(x, p['norm2_g']), conv_prefix, p['f_up'], p['f_conv_w'], p['f_conv_b'], p['f_down'])
    x = x + f
    return x, k, v, va, c_ext[:, L:], conv_tail


def setup_inputs(seed: int = 0) -> dict:
    key = jax.random.key(seed)
    ks = jax.random.split(key, 32)
    f32 = jnp.float32
    n_pages = PAST_LEN // PAGE_SIZE
    n_used = DEC_BATCH * n_pages
    n_pool = n_used + max(1, n_used // 4)

    def nrm(k, shape, scale=1.0):
        return jax.random.normal(k, shape, f32) * scale

    def gain(k, shape):
        return 1.0 + 0.02 * jax.random.normal(k, shape, f32)

    page_table = jax.random.permutation(ks[0], n_pool)[:n_used].reshape(DEC_BATCH, n_pages).astype(jnp.int32)
    kv_shape = (DEPTH, n_pool, PAGE_SIZE, N_HEADS_B, 2 * HEAD_DIM_B)
    return {
        'x_prompt': nrm(ks[1], (BATCH, SEQ, D_MODEL)),
        'x_sample': nrm(ks[2], (DEC_BATCH, DEC_SEQ, D_MODEL)),
        'cache_k': nrm(ks[3], kv_shape),
        'cache_v': nrm(ks[4], kv_shape),
        'page_table': page_table,
        'state_pool': nrm(ks[5], (DEPTH, DEC_BATCH, MAX_WIN - 1, W_C)),
        'state_conv': nrm(ks[6], (DEPTH, DEC_BATCH, CONV_W - 1, 2 * D_FF)),
        'norm1_g': gain(ks[7], (DEPTH, D_MODEL)),
        'w_in': nrm(ks[8], (DEPTH, D_MODEL, N_IN), D_MODEL ** -0.5),
        'a_vnorm_g': gain(ks[9], (DEPTH, W_A)),
        'a_ws': nrm(ks[10], (DEPTH, G_A, CHUNK, CHUNK), CHUNK ** -0.5),
        'a_bs': 1.0 + nrm(ks[11], (DEPTH, G_A, CHUNK), 0.1),
        'b_qnorm_g': gain(ks[12], (DEPTH, HEAD_DIM_B)),
        'b_knorm_g': gain(ks[13], (DEPTH, HEAD_DIM_B)),
        'b_lq1': nrm(ks[14], (DEPTH, HEAD_DIM_B), 0.1),
        'b_lk1': nrm(ks[15], (DEPTH, HEAD_DIM_B), 0.1),
        'b_lq2': nrm(ks[16], (DEPTH, HEAD_DIM_B), 0.1),
        'b_lk2': nrm(ks[17], (DEPTH, HEAD_DIM_B), 0.1),
        'b_subln_g': gain(ks[18], (DEPTH, 2 * HEAD_DIM_B)),
        'c_w': nrm(ks[19], (DEPTH, G_C, CG_C, CG_C), CG_C ** -0.5),
        'c_scale': gain(ks[20], (DEPTH, W_C)),
        'p_a': nrm(ks[21], (DEPTH, W_A, D_MODEL), W_A ** -0.5),
        'p_b': nrm(ks[22], (DEPTH, W_B, D_MODEL), W_B ** -0.5),
        'p_c': nrm(ks[23], (DEPTH, W_C, D_MODEL), W_C ** -0.5),
        'w_o': nrm(ks[24], (DEPTH, D_MODEL, D_MODEL), D_MODEL ** -0.5),
        'norm2_g': gain(ks[25], (DEPTH, D_MODEL)),
        'f_up': nrm(ks[26], (DEPTH, D_MODEL, 2 * D_FF), D_MODEL ** -0.5),
        'f_conv_w': nrm(ks[27], (DEPTH, CONV_W, 2 * D_FF), CONV_W ** -0.5),
        'f_conv_b': nrm(ks[28], (DEPTH, 2 * D_FF), 0.01),
        'f_down': nrm(ks[29], (DEPTH, D_FF, D_MODEL), D_FF ** -0.5),
    }


def reference(x_prompt, x_sample, cache_k, cache_v, page_table, state_pool, state_conv,
              norm1_g, w_in, a_vnorm_g, a_ws, a_bs, b_qnorm_g, b_knorm_g, b_lq1, b_lk1, b_lq2, b_lk2,
              b_subln_g, c_w, c_scale, p_a, p_b, p_c, w_o, norm2_g, f_up, f_conv_w, f_conv_b, f_down):
    n_dec, n_pages = page_table.shape
    past_len = n_pages * cache_k.shape[2]
    b_p = x_prompt.shape[0]
    pool_zero = jnp.zeros((b_p, MAX_WIN - 1, W_C), x_prompt.dtype)
    conv_zero = jnp.zeros((b_p, CONV_W - 1, 2 * D_FF), x_prompt.dtype)
    y_p, y_s = x_prompt, x_sample
    kp_l, vp_l, ks_l, vs_l, cvs_l, pp_l, ps_l, cp_l, cs_l = [], [], [], [], [], [], [], [], []
    for l in range(DEPTH):
        p = {
            'norm1_g': norm1_g[l], 'w_in': w_in[l], 'a_vnorm_g': a_vnorm_g[l], 'a_ws': a_ws[l], 'a_bs': a_bs[l],
            'b_qnorm_g': b_qnorm_g[l], 'b_knorm_g': b_knorm_g[l], 'b_lq1': b_lq1[l], 'b_lk1': b_lk1[l],
            'b_lq2': b_lq2[l], 'b_lk2': b_lk2[l], 'b_subln_g': b_subln_g[l], 'c_w': c_w[l], 'c_scale': c_scale[l],
            'p_a': p_a[l], 'p_b': p_b[l], 'p_c': p_c[l], 'w_o': w_o[l], 'norm2_g': norm2_g[l],
            'f_up': f_up[l], 'f_conv_w': f_conv_w[l], 'f_conv_b': f_conv_b[l], 'f_down': f_down[l],
        }
        lam_init = 0.8 - 0.6 * math.exp(-0.3 * l)
        y_p, k_p, v_p, _, pool_p, conv_p = hybrid_layer(y_p, p, lam_init, None, None, pool_zero, conv_zero, 0)
        k_past = cache_k[l][page_table].reshape(n_dec, past_len, N_HEADS_B, 2 * HEAD_DIM_B)
        v_past = cache_v[l][page_table].reshape(n_dec, past_len, N_HEADS_B, 2 * HEAD_DIM_B)
        y_s, k_s, v_s, cv_s, pool_s, conv_s = hybrid_layer(y_s, p, lam_init, k_past, v_past,
                                                            state_pool[l], state_conv[l], past_len)
        kp_l.append(k_p); vp_l.append(v_p); ks_l.append(k_s); vs_l.append(v_s); cvs_l.append(cv_s)
        pp_l.append(pool_p); ps_l.append(pool_s); cp_l.append(conv_p); cs_l.append(conv_s)
    return (y_p, y_s, jnp.stack(kp_l), jnp.stack(vp_l), jnp.stack(ks_l), jnp.stack(vs_l), jnp.stack(cvs_l),
            jnp.stack(pp_l), jnp.stack(ps_l), jnp.stack(cp_l), jnp.stack(cs_l))
```

```python
import functools
import math

import jax
import jax.numpy as jnp
from jax import lax
from jax.experimental import pallas as pl
from jax.experimental.pallas import tpu as pltpu

F32 = jnp.float32
BF16 = jnp.bfloat16
EPS = 1e-6
NEG_INF = -1e30
POOL_WINDOWS = (2, 4, 8, 16)
POOL_HALO = 16
CONV_HALO = 16
CONV_W = 3
LANES = 128
VMEM_LIMIT = 48 * 1024 * 1024


def _cparams(sem):
    return pltpu.CompilerParams(dimension_semantics=sem, vmem_limit_bytes=VMEM_LIMIT)


def _rms(xf):
    return xf * lax.rsqrt(jnp.mean(xf * xf, axis=-1, keepdims=True) + EPS)


def _inproj_kernel(x_ref, g1_ref, w_ref, avg_ref, qg_ref, kg_ref, ones_ref, *refs, hd, prompt, tk):
    if prompt:
        ua_ref, q_ref, k32_ref, kb_ref, v32_ref, vt_ref, c_ref, g_ref, h_ref = refs
    else:
        ua_ref, q_ref, k32_ref, v32_ref, c_ref, g_ref, h_ref = refs
    j = pl.program_id(1)

    @pl.when(j == 0)
    def _():
        h_ref[...] = (_rms(x_ref[...]) * g1_ref[...]).astype(BF16)

    z = jnp.dot(h_ref[...], w_ref[...], preferred_element_type=F32)

    def group_rms(zz):
        sq = zz * zz
        hi = sq.astype(BF16)
        lo = (sq - hi.astype(F32)).astype(BF16)
        ss = (jnp.dot(hi, ones_ref[...], preferred_element_type=F32)
              + jnp.dot(lo, ones_ref[...], preferred_element_type=F32))
        return zz * lax.rsqrt(ss * (1.0 / hd) + EPS)

    @pl.when(j == 0)
    def _():
        ua_ref[...] = jax.nn.gelu(z)

    @pl.when(j == 1)
    def _():
        ua_ref[...] = _rms(jax.nn.gelu(z)) * avg_ref[...]

    @pl.when(j == 2)
    def _():
        q_ref[...] = (group_rms(z) * qg_ref[...] * (hd ** -0.5)).astype(q_ref.dtype)

    @pl.when(j == 3)
    def _():
        kn = group_rms(z) * kg_ref[...]
        k32_ref[...] = kn
        if prompt:
            kb_ref[...] = kn.astype(BF16)

    @pl.when(j == 4)
    def _():
        v32_ref[...] = z
        if prompt:
            zt = z.T.astype(BF16)
            for t in range(vt_ref.shape[0]):
                vt_ref[t] = zt[:, t * tk:(t + 1) * tk]

    @pl.when(j == 5)
    def _():
        c_ref[...] = z

    @pl.when(j >= 6)
    def _():
        g_ref[...] = jax.nn.sigmoid(z)


def _inproj(x, g1, w_bf, avg, qg, kg, ones_bd, *, tm, hd, prompt, tk):
    T, D = x.shape
    n_in = w_bf.shape[1]
    tn = 512
    nj = n_in // tn
    ng = nj - 6
    grid = (T // tm, nj)
    row = lambda i, j: (i, 0)
    const = lambda i, j: (0, 0)
    in_specs = [
        pl.BlockSpec((tm, D), row),
        pl.BlockSpec((1, D), const),
        pl.BlockSpec((D, tn), lambda i, j: (0, j)),
        pl.BlockSpec((1, tn), const),
        pl.BlockSpec((1, tn), const),
        pl.BlockSpec((1, tn), const),
        pl.BlockSpec((tn, tn), const),
    ]
    blk = pl.BlockSpec((tm, tn), row)
    ua_spec = pl.BlockSpec((tm, tn), lambda i, j: (i, jnp.minimum(j, 1)))
    g_spec = pl.BlockSpec((tm, tn), lambda i, j: (i, jnp.clip(j - 6, 0, ng - 1)))
    sd = jax.ShapeDtypeStruct
    if prompt:
        out_shape = [sd((T, 2 * tn), F32), sd((T, tn), BF16), sd((T, tn), F32), sd((T, tn), BF16),
                     sd((T, tn), F32), sd((T // tk, tn, tk), BF16), sd((T, tn), F32), sd((T, ng * tn), F32)]
        vt_spec = pl.BlockSpec((tm // tk, tn, tk), lambda i, j: (i, 0, 0))
        out_specs = [ua_spec, blk, blk, blk, blk, vt_spec, blk, g_spec]
    else:
        out_shape = [sd((T, 2 * tn), F32), sd((T, tn), F32), sd((T, tn), F32),
                     sd((T, tn), F32), sd((T, tn), F32), sd((T, ng * tn), F32)]
        out_specs = [ua_spec, blk, blk, blk, blk, g_spec]
    return pl.pallas_call(
        functools.partial(_inproj_kernel, hd=hd, prompt=prompt, tk=tk),
        grid=grid, in_specs=in_specs, out_specs=out_specs, out_shape=out_shape,
        scratch_shapes=[pltpu.VMEM((tm, D), BF16)],
        compiler_params=_cparams(("arbitrary", "arbitrary")),
        name="inproj_prompt" if prompt else "inproj_sample",
    )(x, g1, w_bf, avg, qg, kg, ones_bd)


def _gating_kernel(u_ref, va_ref, ws_ref, bst_ref, o_ref, *, n_chunks, cg):
    n_groups, ch, _ = ws_ref.shape
    r_i = lax.broadcasted_iota(jnp.int32, (ch, ch), 0)
    c_i = lax.broadcasted_iota(jnp.int32, (ch, ch), 1)
    tri = r_i >= c_i
    for g in range(n_groups):
        wm = jnp.where(tri, ws_ref[g], 0.0).astype(BF16)
        bcol = bst_ref[:, g:g + 1]
        cols = slice(g * cg, (g + 1) * cg)
        for r in range(n_chunks):
            rows = slice(r * ch, (r + 1) * ch)
            s = jnp.dot(wm, va_ref[rows, cols].astype(BF16), preferred_element_type=F32) + bcol
            o_ref[rows, cols] = (u_ref[rows, cols] * s).astype(o_ref.dtype)


def _gating(ua, ws, bst, *, tm):
    T = ua.shape[0]
    wa = ua.shape[1] // 2
    n_groups, ch, _ = ws.shape
    return pl.pallas_call(
        functools.partial(_gating_kernel, n_chunks=tm // ch, cg=wa // n_groups),
        grid=(T // tm,),
        in_specs=[pl.BlockSpec((tm, wa), lambda i: (i, 0)),
                  pl.BlockSpec((tm, wa), lambda i: (i, 1)),
                  pl.BlockSpec(ws.shape, lambda i: (0, 0, 0)),
                  pl.BlockSpec(bst.shape, lambda i: (0, 0))],
        out_specs=pl.BlockSpec((tm, wa), lambda i: (i, 0)),
        out_shape=jax.ShapeDtypeStruct((T, wa), BF16),
        compiler_params=_cparams(("arbitrary",)),
        name="gating_prompt",
    )(ua, ua, ws, bst)


def _lambda(lq1_ref, lk1_ref, lq2_ref, lk2_ref, lam_init):
    a = jnp.sum(lq1_ref[...] * lk1_ref[...], axis=-1, keepdims=True)
    b = jnp.sum(lq2_ref[...] * lk2_ref[...], axis=-1, keepdims=True)
    return jnp.exp(a) - jnp.exp(b) + lam_init


def _attn_kernel(q_ref, k_ref, vt_ref, lq1_ref, lk1_ref, lq2_ref, lk2_ref, sg_ref, o_ref,
                 acc_ref, *, tq, hd, lam_init):
    qi = pl.program_id(2)
    q = q_ref[...]
    lane = lax.broadcasted_iota(jnp.int32, q.shape, 1)
    zero = jnp.zeros_like(q)
    qc = (jnp.where(lane < hd, q, zero), jnp.where(lane >= hd, q, zero))
    acc_ref[...] = jnp.zeros_like(acc_ref)
    r_i = lax.broadcasted_iota(jnp.int32, (tq, tq), 0)
    c_i = lax.broadcasted_iota(jnp.int32, (tq, tq), 1)
    causal = r_i <= c_i

    def block(kj, stats, masked):
        kblk = k_ref[pl.ds(pl.multiple_of(kj * tq, tq), tq), :]
        vblk = vt_ref[kj]
        new = []
        for c in range(2):
            m, l = stats[c]
            s = lax.dot_general(kblk, qc[c], (((1,), (1,)), ((), ())), preferred_element_type=F32)
            if masked:
                s = jnp.where(causal, s, NEG_INF)
            mn = jnp.maximum(m, jnp.max(s, axis=0, keepdims=True))
            p = jnp.exp(s - mn)
            alpha = jnp.exp(m - mn)
            l = alpha * l + jnp.sum(p, axis=0, keepdims=True)
            acc_ref[c] = alpha * acc_ref[c] + jnp.dot(vblk, p.astype(BF16), preferred_element_type=F32)
            new.append((mn, l))
        return tuple(new)

    init = tuple((jnp.full((1, tq), NEG_INF, F32), jnp.zeros((1, tq), F32)) for _ in range(2))
    stats = lax.fori_loop(0, qi, lambda kj, st: block(kj, st, False), init)
    stats = block(qi, stats, True)
    lam = _lambda(lq1_ref, lk1_ref, lq2_ref, lk2_ref, lam_init)
    o = acc_ref[0] / stats[0][1] - lam * (acc_ref[1] / stats[1][1])
    o = o * lax.rsqrt(jnp.mean(o * o, axis=0, keepdims=True) + EPS) * sg_ref[...]
    o_ref[...] = (o * (1.0 - lam_init)).T.astype(o_ref.dtype)


def _attn_prompt(q, kb, vt, lq1, lk1, lq2, lk2, sg_col, *, n_batch, seq, n_heads, hd, tq, lam_init):
    T, wb = q.shape
    dh = 2 * hd
    nq = seq // tq
    small = lambda b, h, i: (0, 0)
    return pl.pallas_call(
        functools.partial(_attn_kernel, tq=tq, hd=hd, lam_init=lam_init),
        grid=(n_batch, n_heads, nq),
        in_specs=[pl.BlockSpec((tq, dh), lambda b, h, i: (b * nq + i, h)),
                  pl.BlockSpec((seq, dh), lambda b, h, i: (b, h)),
                  pl.BlockSpec((nq, dh, tq), lambda b, h, i: (b, h, 0)),
                  pl.BlockSpec((1, hd), small), pl.BlockSpec((1, hd), small),
                  pl.BlockSpec((1, hd), small), pl.BlockSpec((1, hd), small),
                  pl.BlockSpec((dh, 1), small)],
        out_specs=pl.BlockSpec((tq, dh), lambda b, h, i: (b * nq + i, h)),
        out_shape=jax.ShapeDtypeStruct((T, wb), BF16),
        scratch_shapes=[pltpu.VMEM((2, dh, tq), F32)],
        compiler_params=_cparams(("arbitrary", "arbitrary", "arbitrary")),
        name="attn_prompt",
    )(q, kb, vt, lq1, lk1, lq2, lk2, sg_col)


def _pool_kernel(c_ref, halo_ref, cw_ref, cs_ref, o_ref, ext_ref, *, tp, tiles_per_seq, cg):
    t_in_seq = pl.program_id(0) % tiles_per_seq
    ext_ref[0:POOL_HALO, :] = jnp.where(t_in_seq == 0, 0.0, halo_ref[...])
    ext_ref[POOL_HALO:, :] = c_ref[...]
    pos = t_in_seq * tp + lax.broadcasted_iota(jnp.int32, (tp, 1), 0)
    for g, win in enumerate(POOL_WINDOWS):
        cols = slice(g * cg, (g + 1) * cg)
        wsum = ext_ref[POOL_HALO:POOL_HALO + tp, cols]
        for jj in range(1, win):
            wsum = wsum + ext_ref[POOL_HALO - jj:POOL_HALO - jj + tp, cols]
        cnt = jnp.minimum(pos + 1, win).astype(F32)
        pooled = wsum / cnt - c_ref[:, cols]
        out = jnp.dot(pooled.astype(BF16), cw_ref[g].astype(BF16), preferred_element_type=F32)
        o_ref[:, cols] = (out * cs_ref[:, cols]).astype(o_ref.dtype)


def _pool_prompt(c, cw, cs, *, seq, tp):
    T, wc = c.shape
    n_groups = cw.shape[0]
    hb = tp // POOL_HALO
    return pl.pallas_call(
        functools.partial(_pool_kernel, tp=tp, tiles_per_seq=seq // tp, cg=wc // n_groups),
        grid=(T // tp,),
        in_specs=[pl.BlockSpec((tp, wc), lambda i: (i, 0)),
                  pl.BlockSpec((POOL_HALO, wc), lambda i: (jnp.maximum(i * hb - 1, 0), 0)),
                  pl.BlockSpec(cw.shape, lambda i: (0, 0, 0)),
                  pl.BlockSpec((1, wc), lambda i: (0, 0))],
        out_specs=pl.BlockSpec((tp, wc), lambda i: (i, 0)),
        out_shape=jax.ShapeDtypeStruct((T, wc), BF16),
        scratch_shapes=[pltpu.VMEM((tp + POOL_HALO, wc), F32)],
        compiler_params=_cparams(("arbitrary",)),
        name="pool_prompt",
    )(c, c, cw, cs)


def _sample_branches_kernel(u_ref, va_ref, w00_ref, b0_ref, c_ref, st_ref, cw_ref, cs_ref,
                            oa_ref, oc_ref, *, cg, past_len):
    oa_ref[...] = (u_ref[...] * (w00_ref[...] * va_ref[...] + b0_ref[...])).astype(oa_ref.dtype)
    n_state = st_ref.shape[0]
    for g, win in enumerate(POOL_WINDOWS):
        cols = slice(g * cg, (g + 1) * cg)
        wsum = c_ref[:, cols]
        for jj in range(1, win):
            wsum = wsum + st_ref[n_state - jj][:, cols]
        pooled = wsum / float(min(past_len + 1, win)) - c_ref[:, cols]
        out = jnp.dot(pooled.astype(BF16), cw_ref[g].astype(BF16), preferred_element_type=F32)
        oc_ref[:, cols] = (out * cs_ref[:, cols]).astype(oc_ref.dtype)


def _sample_branches(ua, w00, b0, c, st_t, cw, cs, *, past_len):
    T, wc = c.shape
    wa = ua.shape[1] // 2
    full2 = lambda a: pl.BlockSpec(a.shape, lambda i: (0,) * a.ndim)
    return pl.pallas_call(
        functools.partial(_sample_branches_kernel, cg=wc // cw.shape[0], past_len=past_len),
        grid=(1,),
        in_specs=[pl.BlockSpec((T, wa), lambda i: (0, 0)), pl.BlockSpec((T, wa), lambda i: (0, 1)),
                  full2(w00), full2(b0), full2(c), full2(st_t), full2(cw), full2(cs)],
        out_specs=[pl.BlockSpec((T, wa), lambda i: (0, 0)), pl.BlockSpec((T, wc), lambda i: (0, 0))],
        out_shape=[jax.ShapeDtypeStruct((T, wa), BF16), jax.ShapeDtypeStruct((T, wc), BF16)],
        compiler_params=_cparams(("arbitrary",)),
        name="branches_sample",
    )(ua, ua, w00, b0, c, st_t, cw, cs)


def _decode_attn_kernel(pt_ref, q_ref, kn_ref, vn_ref, lq1_ref, lk1_ref, lq2_ref, lk2_ref, sg_ref,
                        *refs, n_pages_step, n_heads, hd, lam_init):
    del pt_ref
    kp = refs[:n_pages_step]
    vp = refs[n_pages_step:2 * n_pages_step]
    o_ref, m_ref, l_ref, acc_ref = refs[2 * n_pages_step:]
    s_idx = pl.program_id(1)
    rows, dh = q_ref.shape
    r_i = lax.broadcasted_iota(jnp.int32, (rows, dh), 0)
    l_i = lax.broadcasted_iota(jnp.int32, (rows, dh), 1)
    comp_sel = (l_i >= hd) == ((r_i // n_heads) % 2 == 1)
    qm = jnp.where(comp_sel, q_ref[...], 0.0)
    qm_b = qm.astype(BF16)

    @pl.when(s_idx == 0)
    def _():
        m_ref[...] = jnp.full_like(m_ref, NEG_INF)
        l_ref[...] = jnp.zeros_like(l_ref)
        acc_ref[...] = jnp.zeros_like(acc_ref)

    page_rows = kp[0].shape[0]
    pr_i = lax.broadcasted_iota(jnp.int32, (rows, page_rows), 0)
    pl_i = lax.broadcasted_iota(jnp.int32, (rows, page_rows), 1)
    head_ok = (pl_i % n_heads) == (pr_i % n_heads)
    scores = []
    for r in range(n_pages_step):
        s = lax.dot_general(qm_b, kp[r][...].astype(BF16), (((1,), (1,)), ((), ())),
                            preferred_element_type=F32)
        scores.append(jnp.where(head_ok, s, NEG_INF))
    m_old = m_ref[...]
    m_new = m_old
    for s in scores:
        m_new = jnp.maximum(m_new, jnp.max(s, axis=-1, keepdims=True))
    alpha = jnp.exp(m_old - m_new)
    l_new = alpha * l_ref[...]
    acc = alpha * acc_ref[...]
    for r in range(n_pages_step):
        p = jnp.exp(scores[r] - m_new)
        l_new = l_new + jnp.sum(p, axis=-1, keepdims=True)
        acc = acc + jnp.dot(p.astype(BF16), vp[r][...].astype(BF16), preferred_element_type=F32)
    m_ref[...] = m_new
    l_ref[...] = l_new
    acc_ref[...] = acc

    @pl.when(s_idx == pl.num_programs(1) - 1)
    def _():
        s_self = jnp.sum(qm * kn_ref[...], axis=-1, keepdims=True)
        m_fin = jnp.maximum(m_new, s_self)
        a2 = jnp.exp(m_new - m_fin)
        p_self = jnp.exp(s_self - m_fin)
        l_fin = a2 * l_new + p_self
        o_all = (a2 * acc + p_self * vn_ref[...]) / l_fin
        lam = _lambda(lq1_ref, lk1_ref, lq2_ref, lk2_ref, lam_init)
        o = o_all[0:n_heads] - lam * o_all[n_heads:2 * n_heads]
        o_ref[...] = _rms(o) * sg_ref[...] * (1.0 - lam_init)


def _attn_decode(page_table, q16, kn16, vn16, lq1, lk1, lq2, lk2, sg_row, cache_k4, cache_v4, *,
                 layer, n_heads, hd, lam_init, n_pages_step):
    nb, rows, dh = q16.shape
    n_pages = page_table.shape[1]
    page_rows = cache_k4.shape[2]
    gp = n_pages_step
    small = lambda b, s, pt: (0, 0)
    per_b = lambda b, s, pt: (b, 0, 0)

    def page_spec(r):
        return pl.BlockSpec((None, None, page_rows, dh), lambda b, s, pt: (layer, pt[b, s * gp + r], 0, 0))

    in_specs = ([pl.BlockSpec((None, rows, dh), per_b)] * 3
                + [pl.BlockSpec((1, hd), small)] * 4 + [pl.BlockSpec((1, dh), small)]
                + [page_spec(r) for r in range(gp)] * 2)
    grid_spec = pltpu.PrefetchScalarGridSpec(
        num_scalar_prefetch=1, grid=(nb, n_pages // gp), in_specs=in_specs,
        out_specs=pl.BlockSpec((None, n_heads, dh), per_b),
        scratch_shapes=[pltpu.VMEM((rows, 1), F32), pltpu.VMEM((rows, 1), F32), pltpu.VMEM((rows, dh), F32)])
    return pl.pallas_call(
        functools.partial(_decode_attn_kernel, n_pages_step=gp, n_heads=n_heads, hd=hd, lam_init=lam_init),
        grid_spec=grid_spec,
        out_shape=jax.ShapeDtypeStruct((nb, n_heads, dh), F32),
        compiler_params=_cparams(("arbitrary", "arbitrary")),
        name="attn_decode",
    )(page_table, q16, kn16, vn16, lq1, lk1, lq2, lk2, sg_row, *([cache_k4] * gp), *([cache_v4] * gp))


def _merge_kernel(x_ref, oa_ref, ob_ref, oc_ref, ga_ref, gb_ref, gc_ref, pa_ref, pb_ref, pc_ref,
                  wo_ref, y_ref):
    m = ga_ref[...] * jnp.dot(oa_ref[...], pa_ref[...], preferred_element_type=F32)
    m = m + gb_ref[...] * jnp.dot(ob_ref[...], pb_ref[...], preferred_element_type=F32)
    m = m + gc_ref[...] * jnp.dot(oc_ref[...], pc_ref[...], preferred_element_type=F32)
    y_ref[...] = x_ref[...] + jnp.dot(m.astype(BF16), wo_ref[...], preferred_element_type=F32)


def _merge(x, oa, ob, oc, g, pa, pb, pc, wo, *, tm, name):
    T, D = x.shape
    row = lambda i: (i, 0)
    const = lambda i: (0, 0)
    return pl.pallas_call(
        _merge_kernel,
        grid=(T // tm,),
        in_specs=[pl.BlockSpec((tm, D), row),
                  pl.BlockSpec((tm, oa.shape[1]), row), pl.BlockSpec((tm, ob.shape[1]), row),
                  pl.BlockSpec((tm, oc.shape[1]), row),
                  pl.BlockSpec((tm, D), lambda i: (i, 0)), pl.BlockSpec((tm, D), lambda i: (i, 1)),
                  pl.BlockSpec((tm, D), lambda i: (i, 2)),
                  pl.BlockSpec(pa.shape, const), pl.BlockSpec(pb.shape, const),
                  pl.BlockSpec(pc.shape, const), pl.BlockSpec(wo.shape, const)],
        out_specs=pl.BlockSpec((tm, D), row),
        out_shape=jax.ShapeDtypeStruct((T, D), F32),
        compiler_params=_cparams(("arbitrary",)),
        name=name,
    )(x, oa, ob, oc, g, g, g, pa, pb, pc, wo)


def _conv3(cb, cw, x2, x1, x0):
    return cb + cw[0:1] * x2 + cw[1:2] * x1 + cw[2:3] * x0


def _ffn_prompt_kernel(x_ref, xh_ref, g2_ref, fug_ref, fuv_ref, cwg_ref, cwv_ref, cbg_ref, cbv_ref,
                       fd_ref, y_ref, tg_ref, tv_ref, h_ref, hh_ref, eg_ref, ev_ref, *, tm, tiles_per_seq):
    j = pl.program_id(1)
    first = (pl.program_id(0) % tiles_per_seq) == 0

    @pl.when(j == 0)
    def _():
        h_ref[...] = (_rms(x_ref[...]) * g2_ref[...]).astype(BF16)
        hh_ref[...] = (_rms(xh_ref[...]) * g2_ref[...]).astype(BF16)

    def up_conv(fu_ref, cw_ref, cb_ref, e_ref, t_ref):
        up = jnp.dot(h_ref[...], fu_ref[...], preferred_element_type=F32)
        uph = jnp.dot(hh_ref[...], fu_ref[...], preferred_element_type=F32)
        e_ref[0:CONV_HALO, :] = jnp.where(first, 0.0, uph)
        e_ref[CONV_HALO:, :] = up
        t_ref[...] = up[tm - 8:tm, :]
        cw = cw_ref[...]
        return _conv3(cb_ref[...], cw, e_ref[CONV_HALO - 2:CONV_HALO - 2 + tm, :],
                      e_ref[CONV_HALO - 1:CONV_HALO - 1 + tm, :], up)

    cg = up_conv(fug_ref, cwg_ref, cbg_ref, eg_ref, tg_ref)
    cv = up_conv(fuv_ref, cwv_ref, cbv_ref, ev_ref, tv_ref)
    act = (jax.nn.silu(cg) * cv).astype(BF16)
    d = jnp.dot(act, fd_ref[...], preferred_element_type=F32)

    @pl.when(j == 0)
    def _():
        y_ref[...] = x_ref[...] + d

    @pl.when(j > 0)
    def _():
        y_ref[...] += d


def _ffn_prompt(x, g2, fu, cw, cb, fd, *, tm, tf, seq):
    T, D = x.shape
    dff = fd.shape[0]
    nf = dff // tf
    hb = tm // CONV_HALO
    row = lambda i, j: (i, 0)
    gcol = lambda i, j: (0, j)
    vcol = lambda i, j: (0, nf + j)
    nt = T // tm
    return pl.pallas_call(
        functools.partial(_ffn_prompt_kernel, tm=tm, tiles_per_seq=seq // tm),
        grid=(nt, nf),
        in_specs=[pl.BlockSpec((tm, D), row),
                  pl.BlockSpec((CONV_HALO, D), lambda i, j: (jnp.maximum(i * hb - 1, 0), 0)),
                  pl.BlockSpec((1, D), lambda i, j: (0, 0)),
                  pl.BlockSpec((D, tf), gcol), pl.BlockSpec((D, tf), vcol),
                  pl.BlockSpec((CONV_W, tf), gcol), pl.BlockSpec((CONV_W, tf), vcol),
                  pl.BlockSpec((1, tf), gcol), pl.BlockSpec((1, tf), vcol),
                  pl.BlockSpec((tf, D), lambda i, j: (j, 0))],
        out_specs=[pl.BlockSpec((tm, D), row),
                   pl.BlockSpec((8, tf), lambda i, j: (i, j)), pl.BlockSpec((8, tf), lambda i, j: (i, j))],
        out_shape=[jax.ShapeDtypeStruct((T, D), F32),
                   jax.ShapeDtypeStruct((nt * 8, dff), F32), jax.ShapeDtypeStruct((nt * 8, dff), F32)],
        scratch_shapes=[pltpu.VMEM((tm, D), BF16), pltpu.VMEM((CONV_HALO, D), BF16),
                        pltpu.VMEM((tm + CONV_HALO, tf), F32), pltpu.VMEM((tm + CONV_HALO, tf), F32)],
        compiler_params=_cparams(("arbitrary", "arbitrary")),
        name="ffn_prompt",
    )(x, x, g2, fu, fu, cw, cw, cb, cb, fd)


def _ffn_sample_kernel(x_ref, g2_ref, fug_ref, fuv_ref, cwg_ref, cwv_ref, cbg_ref, cbv_ref,
                       p0g_ref, p0v_ref, p1g_ref, p1v_ref, fd_ref, y_ref, ug_ref, uv_ref, h_ref):
    j = pl.program_id(0)

    @pl.when(j == 0)
    def _():
        h_ref[...] = (_rms(x_ref[...]) * g2_ref[...]).astype(BF16)

    upg = jnp.dot(h_ref[...], fug_ref[...], preferred_element_type=F32)
    upv = jnp.dot(h_ref[...], fuv_ref[...], preferred_element_type=F32)
    ug_ref[...] = upg
    uv_ref[...] = upv
    cg = _conv3(cbg_ref[...], cwg_ref[...], p0g_ref[...], p1g_ref[...], upg)
    cv = _conv3(cbv_ref[...], cwv_ref[...], p0v_ref[...], p1v_ref[...], upv)
    d = jnp.dot((jax.nn.silu(cg) * cv).astype(BF16), fd_ref[...], preferred_element_type=F32)

    @pl.when(j == 0)
    def _():
        y_ref[...] = x_ref[...] + d

    @pl.when(j > 0)
    def _():
        y_ref[...] += d


def _ffn_sample(x, g2, fu, cw, cb, pre0, pre1, fd, *, tf):
    T, D = x.shape
    dff = fd.shape[0]
    nf = dff // tf
    gcol = lambda j: (0, j)
    vcol = lambda j: (0, nf + j)
    return pl.pallas_call(
        _ffn_sample_kernel,
        grid=(nf,),
        in_specs=[pl.BlockSpec((T, D), lambda j: (0, 0)), pl.BlockSpec((1, D), lambda j: (0, 0)),
                  pl.BlockSpec((D, tf), gcol), pl.BlockSpec((D, tf), vcol),
                  pl.BlockSpec((CONV_W, tf), gcol), pl.BlockSpec((CONV_W, tf), vcol),
                  pl.BlockSpec((1, tf), gcol), pl.BlockSpec((1, tf), vcol),
                  pl.BlockSpec((T, tf), gcol), pl.BlockSpec((T, tf), vcol),
                  pl.BlockSpec((T, tf), gcol), pl.BlockSpec((T, tf), vcol),
                  pl.BlockSpec((tf, D), lambda j: (j, 0))],
        out_specs=[pl.BlockSpec((T, D), lambda j: (0, 0)),
                   pl.BlockSpec((T, tf), gcol), pl.BlockSpec((T, tf), gcol)],
        out_shape=[jax.ShapeDtypeStruct((T, D), F32),
                   jax.ShapeDtypeStruct((T, dff), F32), jax.ShapeDtypeStruct((T, dff), F32)],
        scratch_shapes=[pltpu.VMEM((T, D), BF16)],
        compiler_params=_cparams(("arbitrary",)),
        name="ffn_sample",
    )(x, g2, fu, fu, cw, cw, cb, cb, pre0, pre0, pre1, pre1, fd)


def _tiles(seq):
    tm = 512
    while seq % tm:
        tm //= 2
    return tm


def _ff_tile(dff):
    tf = 256
    while dff % tf:
        tf //= 2
    return tf


def kernel(x_prompt, x_sample, cache_k, cache_v, page_table, state_pool, state_conv, norm1_g, w_in, a_vnorm_g, a_ws, a_bs, b_qnorm_g, b_knorm_g, b_lq1, b_lk1, b_lq2, b_lk2, b_subln_g, c_w, c_scale, p_a, p_b, p_c, w_o, norm2_g, f_up, f_conv_w, f_conv_b, f_down):
    n_batch, seq, d_model = x_prompt.shape
    n_dec, dec_seq, _ = x_sample.shape
    depth, n_pool, page_size, n_heads, dh = cache_k.shape
    hd = dh // 2
    wb = n_heads * dh
    wa = a_vnorm_g.shape[-1]
    wc = c_scale.shape[-1]
    n_groups_a, chunk = a_ws.shape[1], a_ws.shape[2]
    dff = f_down.shape[1]
    n_pages = page_table.shape[1]
    past_len = n_pages * page_size
    n_state = state_pool.shape[2]
    assert dec_seq == 1 and wa == wb == wc == 512 and d_model == 2 * wa and dh == LANES
    assert past_len % chunk == 0 and n_state == POOL_WINDOWS[-1] - 1 and state_conv.shape[2] == CONV_W - 1
    tm = _tiles(seq)
    assert tm % chunk == 0 and tm >= POOL_HALO
    tf = _ff_tile(dff)
    pages_step = 8
    while n_pages % pages_step:
        pages_step //= 2

    w_in_b, p_a_b, p_b_b, p_c_b, w_o_b, f_up_b, f_down_b = (
        a.astype(BF16) for a in (w_in, p_a, p_b, p_c, w_o, f_up, f_down))
    gi = jnp.arange(wb) // hd
    ones_bd = (gi[:, None] == gi[None, :]).astype(BF16)
    cache_k4 = cache_k.reshape(depth, n_pool, page_size * n_heads, dh)
    cache_v4 = cache_v.reshape(depth, n_pool, page_size * n_heads, dh)

    T = n_batch * seq
    y_p = x_prompt.reshape(T, d_model)
    y_s = x_sample.reshape(n_dec, d_model)
    row2 = lambda a: a.reshape(1, -1)
    outs = [[] for _ in range(9)]
    for l in range(depth):
        lam_init = 0.8 - 0.6 * math.exp(-0.3 * l)
        g1, avg, g2 = row2(norm1_g[l]), row2(a_vnorm_g[l]), row2(norm2_g[l])
        qg = row2(jnp.tile(b_qnorm_g[l], wb // hd))
        kg = row2(jnp.tile(b_knorm_g[l], wb // hd))
        lams = (row2(b_lq1[l]), row2(b_lk1[l]), row2(b_lq2[l]), row2(b_lk2[l]))
        cs, cb = row2(c_scale[l]), row2(f_conv_b[l])

        ua, q, k32, kb, v32, vt, c, g = _inproj(y_p, g1, w_in_b[l], avg, qg, kg, ones_bd,
                                                tm=tm, hd=hd, prompt=True, tk=tm)
        oa = _gating(ua, a_ws[l], a_bs[l].T, tm=tm)
        ob = _attn_prompt(q, kb, vt, *lams, b_subln_g[l].reshape(dh, 1), n_batch=n_batch, seq=seq,
                          n_heads=n_heads, hd=hd, tq=tm, lam_init=lam_init)
        oc = _pool_prompt(c, c_w[l], cs, seq=seq, tp=tm)
        x1 = _merge(y_p, oa, ob, oc, g, p_a_b[l], p_b_b[l], p_c_b[l], w_o_b[l], tm=tm, name="merge_prompt")
        y_p, tg, tv = _ffn_prompt(x1, g2, f_up_b[l], f_conv_w[l], cb, f_down_b[l], tm=tm, tf=tf, seq=seq)
        tail = jnp.concatenate([tg, tv], axis=-1).reshape(n_batch, seq // tm, 8, 2 * dff)
        outs[0].append(k32.reshape(n_batch, seq, n_heads, dh))
        outs[1].append(v32.reshape(n_batch, seq, n_heads, dh))
        outs[5].append(c.reshape(n_batch, seq, wc)[:, seq - n_state:])
        outs[7].append(tail[:, -1, 8 - (CONV_W - 1):])

        ua_s, q_s, k_s, v_s, c_s, g_s = _inproj(y_s, g1, w_in_b[l], avg, qg, kg, ones_bd,
                                                tm=n_dec, hd=hd, prompt=False, tk=n_dec)

        def rows16(a):
            a = a.reshape(n_dec, 1, n_heads, dh)
            a = jnp.broadcast_to(a, (n_dec, 2, n_heads, dh)).reshape(n_dec, 2 * n_heads, dh)
            return jnp.pad(a, ((0, 0), (0, 16 - 2 * n_heads), (0, 0)))

        ob_s = _attn_decode(page_table, rows16(q_s), rows16(k_s), rows16(v_s), *lams, row2(b_subln_g[l]),
                            cache_k4, cache_v4, layer=l, n_heads=n_heads, hd=hd, lam_init=lam_init,
                            n_pages_step=pages_step)
        w00 = row2(jnp.repeat(a_ws[l][:, 0, 0], wa // n_groups_a))
        b0 = row2(jnp.repeat(a_bs[l][:, 0], wa // n_groups_a))
        oa_s, oc_s = _sample_branches(ua_s, w00, b0, c_s, state_pool[l].transpose(1, 0, 2), c_w[l], cs,
                                      past_len=past_len)
        x1_s = _merge(y_s, oa_s, ob_s.reshape(n_dec, wb).astype(BF16), oc_s, g_s,
                      p_a_b[l], p_b_b[l], p_c_b[l], w_o_b[l], tm=n_dec, name="merge_sample")
        y_s, ug, uv = _ffn_sample(x1_s, g2, f_up_b[l], f_conv_w[l], cb,
                                  state_conv[l][:, 0], state_conv[l][:, 1], f_down_b[l], tf=tf)
        up_s = jnp.concatenate([ug, uv], axis=-1)
        outs[2].append(k_s.reshape(n_dec, 1, n_heads, dh))
        outs[3].append(v_s.reshape(n_dec, 1, n_heads, dh))
        outs[4].append(ua_s[:, wa:].reshape(n_dec, 1, wa))
        outs[6].append(jnp.concatenate([state_pool[l][:, 1:], c_s[:, None]], axis=1))
        outs[8].append(jnp.stack([state_conv[l][:, 1], up_s], axis=1))

    st = [jnp.stack(o) for o in outs]
    return (y_p.reshape(n_batch, seq, d_model), y_s.reshape(n_dec, 1, d_model),
            st[0], st[1], st[2], st[3], st[4], st[5], st[6], st[7], st[8])
```

```python
import functools
import math

import jax
import jax.numpy as jnp
from jax import lax
from jax.experimental import pallas as pl
from jax.experimental.pallas import tpu as pltpu

F32 = jnp.float32
BF16 = jnp.bfloat16
EPS = 1e-6
NEG_INF = -1e30
LOG2E = math.log2(math.e)
POOL_WINDOWS = (2, 4, 8, 16)
POOL_HALO = 16
CONV_HALO = 16
CONV_W = 3
LANES = 128
SUBLANES = 8
SEG = 512
VMEM_LIMIT = 56 * 1024 * 1024


def _cparams(sem):
    return pltpu.CompilerParams(dimension_semantics=sem, vmem_limit_bytes=VMEM_LIMIT)


def _resident(shape, index_map):
    return pl.BlockSpec(shape, index_map, pipeline_mode=pl.Buffered(1))


def _rms(xf):
    return xf * lax.rsqrt(jnp.mean(xf * xf, axis=-1, keepdims=True) + EPS)


def _sigmoid(z):
    return 0.5 * jnp.tanh(0.5 * z) + 0.5


def _inproj_kernel(x_ref, g1_ref, w_ref, avg_ref, qg_ref, kg_ref, ones_ref, *refs, hd, n_heads, prompt, tk):
    if prompt:
        ua_ref, q_ref, k32_ref, kb_ref, v32_ref, vt_ref, c_ref, h_ref = refs
    else:
        ua_ref, q_ref, k32_ref, v32_ref, c_ref, h_ref = refs
    tm = x_ref.shape[0]
    dh = 2 * hd
    h_ref[...] = (_rms(x_ref[...]) * g1_ref[...]).astype(BF16)

    def seg(j):
        return jnp.dot(h_ref[...], w_ref[:, j * SEG:(j + 1) * SEG], preferred_element_type=F32)

    def group_rms(zz):
        sq = zz * zz
        hi = sq.astype(BF16)
        lo = (sq - hi.astype(F32)).astype(BF16)
        ss = (jnp.dot(hi, ones_ref[...], preferred_element_type=F32)
              + jnp.dot(lo, ones_ref[...], preferred_element_type=F32))
        return zz * lax.rsqrt(ss * (1.0 / hd) + EPS)

    def head_rows(dst_ref, val):
        for hh in range(n_heads):
            dst_ref[pl.ds(hh, tm, stride=n_heads), :] = val[:, hh * dh:(hh + 1) * dh]

    ua_ref[:, 0:SEG] = jax.nn.gelu(seg(0))
    ua_ref[:, SEG:2 * SEG] = _rms(jax.nn.gelu(seg(1))) * avg_ref[...]
    q_ref[...] = (group_rms(seg(2)) * qg_ref[...] * (hd ** -0.5 * LOG2E)).astype(q_ref.dtype)
    kn = group_rms(seg(3)) * kg_ref[...]
    zv = seg(4)
    if prompt:
        head_rows(k32_ref, kn)
        kb_ref[...] = kn.astype(BF16)
        head_rows(v32_ref, zv)
        zt = zv.T.astype(BF16)
        for t in range(vt_ref.shape[0]):
            vt_ref[t] = zt[:, t * tk:(t + 1) * tk]
    else:
        k32_ref[...] = kn
        v32_ref[...] = zv
    c_ref[...] = seg(5)


def _inproj(x, g1, w_bf, avg, qg, kg, ones_bd, *, tm, hd, n_heads, prompt, tk):
    T, D = x.shape
    row = lambda i: (i, 0)
    const = lambda i: (0, 0)
    in_specs = [pl.BlockSpec((tm, D), row), _resident((1, D), const),
                _resident((D, 6 * SEG), const),
                _resident((1, SEG), const), _resident((1, SEG), const), _resident((1, SEG), const),
                _resident((SEG, SEG), const)]
    blk = pl.BlockSpec((tm, SEG), row)
    sd = jax.ShapeDtypeStruct
    if prompt:
        dh = SEG // n_heads
        hblk = pl.BlockSpec((tm * n_heads, dh), row)
        out_shape = [sd((T, 2 * SEG), F32), sd((T, SEG), BF16), sd((T * n_heads, dh), F32), sd((T, SEG), BF16),
                     sd((T * n_heads, dh), F32), sd((T // tk, SEG, tk), BF16), sd((T, SEG), F32)]
        out_specs = [pl.BlockSpec((tm, 2 * SEG), row), blk, hblk, blk, hblk,
                     pl.BlockSpec((tm // tk, SEG, tk), lambda i: (i, 0, 0)), blk]
    else:
        out_shape = [sd((T, 2 * SEG), F32), sd((T, SEG), F32), sd((T, SEG), F32), sd((T, SEG), F32),
                     sd((T, SEG), F32)]
        out_specs = [pl.BlockSpec((tm, 2 * SEG), row), blk, blk, blk, blk]
    return pl.pallas_call(
        functools.partial(_inproj_kernel, hd=hd, n_heads=n_heads, prompt=prompt, tk=tk),
        grid=(T // tm,), in_specs=in_specs, out_specs=out_specs, out_shape=out_shape,
        scratch_shapes=[pltpu.VMEM((tm, D), BF16)],
        compiler_params=_cparams(("arbitrary",)),
        name="inproj_prompt" if prompt else "inproj_sample",
    )(x, g1, w_bf, avg, qg, kg, ones_bd)


def _gating_kernel(u_ref, va_ref, ws_ref, bst_ref, o_ref, *, n_chunks, cg):
    n_groups, ch, _ = ws_ref.shape
    r_i = lax.broadcasted_iota(jnp.int32, (ch, ch), 0)
    c_i = lax.broadcasted_iota(jnp.int32, (ch, ch), 1)
    tri = r_i >= c_i
    for g in range(n_groups):
        wm = jnp.where(tri, ws_ref[g], 0.0).astype(BF16)
        bcol = bst_ref[:, g:g + 1]
        cols = slice(g * cg, (g + 1) * cg)
        for r in range(n_chunks):
            rows = slice(r * ch, (r + 1) * ch)
            s = jnp.dot(wm, va_ref[rows, cols].astype(BF16), preferred_element_type=F32) + bcol
            o_ref[rows, cols] = (u_ref[rows, cols] * s).astype(o_ref.dtype)


def _gating(ua, ws, bst, *, tm):
    T = ua.shape[0]
    wa = ua.shape[1] // 2
    n_groups, ch, _ = ws.shape
    return pl.pallas_call(
        functools.partial(_gating_kernel, n_chunks=tm // ch, cg=wa // n_groups),
        grid=(T // tm,),
        in_specs=[pl.BlockSpec((tm, wa), lambda i: (i, 0)),
                  pl.BlockSpec((tm, wa), lambda i: (i, 1)),
                  pl.BlockSpec(ws.shape, lambda i: (0, 0, 0)),
                  pl.BlockSpec(bst.shape, lambda i: (0, 0))],
        out_specs=pl.BlockSpec((tm, wa), lambda i: (i, 0)),
        out_shape=jax.ShapeDtypeStruct((T, wa), BF16),
        compiler_params=_cparams(("arbitrary",)),
        name="gating_prompt",
    )(ua, ua, ws, bst)


def _lambda(lq1_ref, lk1_ref, lq2_ref, lk2_ref, lam_init):
    a = jnp.sum(lq1_ref[...] * lk1_ref[...], axis=-1, keepdims=True)
    b = jnp.sum(lq2_ref[...] * lk2_ref[...], axis=-1, keepdims=True)
    return jnp.exp(a) - jnp.exp(b) + lam_init


def _attn_kernel(q_ref, k_ref, vt_ref, lq1_ref, lk1_ref, lq2_ref, lk2_ref, sg_ref, o_ref,
                 acc_ref, *, tq, hd, heads_step, lam_init):
    qi = pl.program_id(2)
    dh = 2 * hd
    lane = lax.broadcasted_iota(jnp.int32, (tq, dh), 1)
    chains = []
    for hh in range(heads_step):
        q = q_ref[:, hh * dh:(hh + 1) * dh]
        zero = jnp.zeros_like(q)
        chains.append((hh, jnp.where(lane < hd, q, zero)))
        chains.append((hh, jnp.where(lane >= hd, q, zero)))
    acc_ref[...] = jnp.zeros_like(acc_ref)
    r_i = lax.broadcasted_iota(jnp.int32, (tq, tq), 0)
    c_i = lax.broadcasted_iota(jnp.int32, (tq, tq), 1)
    causal = r_i <= c_i

    def block(kj, stats, masked):
        row0 = pl.multiple_of(kj * tq, tq)

        def scores(n):
            hh, qc = chains[n]
            kblk = k_ref[pl.ds(row0, tq), hh * dh:(hh + 1) * dh]
            s = lax.dot_general(kblk, qc, (((1,), (1,)), ((), ())), preferred_element_type=F32)
            return jnp.where(causal, s, NEG_INF) if masked else s

        def weighted_values(n, p, alpha):
            hh = chains[n][0]
            vblk = vt_ref[kj, hh * dh:(hh + 1) * dh, :]
            acc_ref[n] = alpha * acc_ref[n] + jnp.dot(vblk, p, preferred_element_type=F32)

        new = []
        s_next = scores(0)
        pending = None
        for n in range(len(chains)):
            s = s_next
            if n + 1 < len(chains):
                s_next = scores(n + 1)
            m, l = stats[n]
            mn = jnp.maximum(m, jnp.max(s, axis=0, keepdims=True))
            p = jnp.exp2(s - mn)
            alpha = jnp.exp2(m - mn)
            new.append((mn, alpha * l + jnp.sum(p, axis=0, keepdims=True)))
            if pending is not None:
                weighted_values(*pending)
            pending = (n, p.astype(BF16), alpha)
        weighted_values(*pending)
        return tuple(new)

    init = tuple((jnp.full((1, tq), NEG_INF, F32), jnp.zeros((1, tq), F32)) for _ in chains)
    stats = lax.fori_loop(0, qi, lambda kj, st: block(kj, st, False), init)
    stats = block(qi, stats, True)
    lam = _lambda(lq1_ref, lk1_ref, lq2_ref, lk2_ref, lam_init)
    for hh in range(heads_step):
        o = acc_ref[2 * hh] / stats[2 * hh][1] - lam * (acc_ref[2 * hh + 1] / stats[2 * hh + 1][1])
        o = o * lax.rsqrt(jnp.mean(o * o, axis=0, keepdims=True) + EPS) * sg_ref[...]
        o_ref[:, hh * dh:(hh + 1) * dh] = (o * (1.0 - lam_init)).T.astype(o_ref.dtype)


def _attn_prompt(q, kb, vt, lq1, lk1, lq2, lk2, sg_col, *, n_batch, seq, n_heads, hd, tq, heads_step,
                 lam_init):
    T, wb = q.shape
    dh = 2 * hd
    nq = seq // tq
    wblk = heads_step * dh
    small = lambda b, h, i: (0, 0)
    return pl.pallas_call(
        functools.partial(_attn_kernel, tq=tq, hd=hd, heads_step=heads_step, lam_init=lam_init),
        grid=(n_batch, n_heads // heads_step, nq),
        in_specs=[pl.BlockSpec((tq, wblk), lambda b, h, i: (b * nq + i, h)),
                  pl.BlockSpec((seq, wblk), lambda b, h, i: (b, h)),
                  pl.BlockSpec((nq, wblk, tq), lambda b, h, i: (b, h, 0)),
                  pl.BlockSpec((1, hd), small), pl.BlockSpec((1, hd), small),
                  pl.BlockSpec((1, hd), small), pl.BlockSpec((1, hd), small),
                  pl.BlockSpec((dh, 1), small)],
        out_specs=pl.BlockSpec((tq, wblk), lambda b, h, i: (b * nq + i, h)),
        out_shape=jax.ShapeDtypeStruct((T, wb), BF16),
        scratch_shapes=[pltpu.VMEM((2 * heads_step, dh, tq), F32)],
        compiler_params=_cparams(("arbitrary", "arbitrary", "arbitrary")),
        name="attn_prompt",
    )(q, kb, vt, lq1, lk1, lq2, lk2, sg_col)


def _pool_kernel(c_ref, halo_ref, cw_ref, cs_ref, o_ref, ext_ref, *, tp, tiles_per_seq, cg):
    t_in_seq = pl.program_id(0) % tiles_per_seq
    ext_ref[0:POOL_HALO, :] = jnp.where(t_in_seq == 0, 0.0, halo_ref[...])
    ext_ref[POOL_HALO:, :] = c_ref[...]
    pos = t_in_seq * tp + lax.broadcasted_iota(jnp.int32, (tp, 1), 0)
    for g, win in enumerate(POOL_WINDOWS):
        cols = slice(g * cg, (g + 1) * cg)
        wsum = ext_ref[POOL_HALO:POOL_HALO + tp, cols]
        for jj in range(1, win):
            wsum = wsum + ext_ref[POOL_HALO - jj:POOL_HALO - jj + tp, cols]
        cnt = jnp.minimum(pos + 1, win).astype(F32)
        pooled = wsum / cnt - c_ref[:, cols]
        out = jnp.dot(pooled.astype(BF16), cw_ref[g].astype(BF16), preferred_element_type=F32)
        o_ref[:, cols] = (out * cs_ref[:, cols]).astype(o_ref.dtype)


def _pool_prompt(c, cw, cs, *, seq, tp):
    T, wc = c.shape
    n_groups = cw.shape[0]
    hb = tp // POOL_HALO
    return pl.pallas_call(
        functools.partial(_pool_kernel, tp=tp, tiles_per_seq=seq // tp, cg=wc // n_groups),
        grid=(T // tp,),
        in_specs=[pl.BlockSpec((tp, wc), lambda i: (i, 0)),
                  pl.BlockSpec((POOL_HALO, wc), lambda i: (jnp.maximum(i * hb - 1, 0), 0)),
                  pl.BlockSpec(cw.shape, lambda i: (0, 0, 0)),
                  pl.BlockSpec((1, wc), lambda i: (0, 0))],
        out_specs=pl.BlockSpec((tp, wc), lambda i: (i, 0)),
        out_shape=jax.ShapeDtypeStruct((T, wc), BF16),
        scratch_shapes=[pltpu.VMEM((tp + POOL_HALO, wc), F32)],
        compiler_params=_cparams(("arbitrary",)),
        name="pool_prompt",
    )(c, c, cw, cs)


def _sample_branches_kernel(u_ref, va_ref, w00_ref, b0_ref, c_ref, st_ref, cw_ref, cs_ref,
                            oa_ref, oc_ref, *, cg, past_len):
    oa_ref[...] = (u_ref[...] * (w00_ref[...] * va_ref[...] + b0_ref[...])).astype(oa_ref.dtype)
    n_state = st_ref.shape[0]
    for g, win in enumerate(POOL_WINDOWS):
        cols = slice(g * cg, (g + 1) * cg)
        wsum = c_ref[:, cols]
        for jj in range(1, win):
            wsum = wsum + st_ref[n_state - jj][:, cols]
        pooled = wsum / float(min(past_len + 1, win)) - c_ref[:, cols]
        out = jnp.dot(pooled.astype(BF16), cw_ref[g].astype(BF16), preferred_element_type=F32)
        oc_ref[:, cols] = (out * cs_ref[:, cols]).astype(oc_ref.dtype)


def _sample_branches(ua, w00, b0, c, st_t, cw, cs, *, past_len):
    T, wc = c.shape
    wa = ua.shape[1] // 2
    full2 = lambda a: pl.BlockSpec(a.shape, lambda i: (0,) * a.ndim)
    return pl.pallas_call(
        functools.partial(_sample_branches_kernel, cg=wc // cw.shape[0], past_len=past_len),
        grid=(1,),
        in_specs=[pl.BlockSpec((T, wa), lambda i: (0, 0)), pl.BlockSpec((T, wa), lambda i: (0, 1)),
                  full2(w00), full2(b0), full2(c), full2(st_t), full2(cw), full2(cs)],
        out_specs=[pl.BlockSpec((T, wa), lambda i: (0, 0)), pl.BlockSpec((T, wc), lambda i: (0, 0))],
        out_shape=[jax.ShapeDtypeStruct((T, wa), BF16), jax.ShapeDtypeStruct((T, wc), BF16)],
        compiler_params=_cparams(("arbitrary",)),
        name="branches_sample",
    )(ua, ua, w00, b0, c, st_t, cw, cs)


def _decode_attn_kernel(pt_ref, q_ref, kn_ref, vn_ref, lq1_ref, lk1_ref, lq2_ref, lk2_ref, sg_ref,
                        *refs, n_pages_step, n_heads, hd, lam_init):
    del pt_ref
    kp = refs[:n_pages_step]
    vp = refs[n_pages_step:2 * n_pages_step]
    o_ref, m_ref, l_ref, acc_ref = refs[2 * n_pages_step:]
    s_idx = pl.program_id(1)
    rows, dh = q_ref.shape
    r_i = lax.broadcasted_iota(jnp.int32, (rows, dh), 0)
    l_i = lax.broadcasted_iota(jnp.int32, (rows, dh), 1)
    comp_sel = (l_i >= hd) == ((r_i // n_heads) % 2 == 1)
    qm = jnp.where(comp_sel, q_ref[...], 0.0)
    qm_b = qm.astype(BF16)

    @pl.when(s_idx == 0)
    def _():
        m_ref[...] = jnp.full_like(m_ref, NEG_INF)
        l_ref[...] = jnp.zeros_like(l_ref)
        acc_ref[...] = jnp.zeros_like(acc_ref)

    page_rows = kp[0].shape[0]
    pr_i = lax.broadcasted_iota(jnp.int32, (rows, page_rows), 0)
    pl_i = lax.broadcasted_iota(jnp.int32, (rows, page_rows), 1)
    head_ok = (pl_i % n_heads) == (pr_i % n_heads)
    scores = []
    for r in range(n_pages_step):
        s = lax.dot_general(qm_b, kp[r][...].astype(BF16), (((1,), (1,)), ((), ())),
                            preferred_element_type=F32)
        scores.append(jnp.where(head_ok, s, NEG_INF))
    m_old = m_ref[...]
    m_new = m_old
    for s in scores:
        m_new = jnp.maximum(m_new, jnp.max(s, axis=-1, keepdims=True))
    alpha = jnp.exp2(m_old - m_new)
    l_new = alpha * l_ref[...]
    acc = alpha * acc_ref[...]
    for r in range(n_pages_step):
        p = jnp.exp2(scores[r] - m_new)
        l_new = l_new + jnp.sum(p, axis=-1, keepdims=True)
        acc = acc + jnp.dot(p.astype(BF16), vp[r][...].astype(BF16), preferred_element_type=F32)
    m_ref[...] = m_new
    l_ref[...] = l_new
    acc_ref[...] = acc

    @pl.when(s_idx == pl.num_programs(1) - 1)
    def _():
        s_self = jnp.sum(qm * kn_ref[...], axis=-1, keepdims=True)
        m_fin = jnp.maximum(m_new, s_self)
        a2 = jnp.exp2(m_new - m_fin)
        p_self = jnp.exp2(s_self - m_fin)
        l_fin = a2 * l_new + p_self
        o_all = (a2 * acc + p_self * vn_ref[...]) / l_fin
        lam = _lambda(lq1_ref, lk1_ref, lq2_ref, lk2_ref, lam_init)
        o = o_all[0:n_heads] - lam * o_all[n_heads:2 * n_heads]
        o_ref[...] = _rms(o) * sg_ref[...] * (1.0 - lam_init)


def _attn_decode(page_table, q16, kn16, vn16, lq1, lk1, lq2, lk2, sg_row, cache_k4, cache_v4, *,
                 layer, n_heads, hd, lam_init, n_pages_step):
    nb, rows, dh = q16.shape
    n_pages = page_table.shape[1]
    page_rows = cache_k4.shape[2]
    gp = n_pages_step
    small = lambda b, s, pt: (0, 0)
    per_b = lambda b, s, pt: (b, 0, 0)

    def page_spec(r):
        return pl.BlockSpec((None, None, page_rows, dh), lambda b, s, pt: (layer, pt[b, s * gp + r], 0, 0))

    in_specs = ([pl.BlockSpec((None, rows, dh), per_b)] * 3
                + [pl.BlockSpec((1, hd), small)] * 4 + [pl.BlockSpec((1, dh), small)]
                + [page_spec(r) for r in range(gp)] * 2)
    grid_spec = pltpu.PrefetchScalarGridSpec(
        num_scalar_prefetch=1, grid=(nb, n_pages // gp), in_specs=in_specs,
        out_specs=pl.BlockSpec((None, n_heads, dh), per_b),
        scratch_shapes=[pltpu.VMEM((rows, 1), F32), pltpu.VMEM((rows, 1), F32), pltpu.VMEM((rows, dh), F32)])
    return pl.pallas_call(
        functools.partial(_decode_attn_kernel, n_pages_step=gp, n_heads=n_heads, hd=hd, lam_init=lam_init),
        grid_spec=grid_spec,
        out_shape=jax.ShapeDtypeStruct((nb, n_heads, dh), F32),
        compiler_params=_cparams(("arbitrary", "arbitrary")),
        name="attn_decode",
    )(page_table, q16, kn16, vn16, lq1, lk1, lq2, lk2, sg_row, *([cache_k4] * gp), *([cache_v4] * gp))


def _merge_kernel(x_ref, oa_ref, ob_ref, oc_ref, g1_ref, wg_ref, pa_ref, pb_ref, pc_ref, wo_ref, y_ref):
    x = x_ref[...]
    D = x.shape[1]
    h = (_rms(x) * g1_ref[...]).astype(BF16)

    def gate(j):
        return _sigmoid(jnp.dot(h, wg_ref[:, j * D:(j + 1) * D], preferred_element_type=F32))

    m = gate(0) * jnp.dot(oa_ref[...], pa_ref[...], preferred_element_type=F32)
    m = m + gate(1) * jnp.dot(ob_ref[...], pb_ref[...], preferred_element_type=F32)
    m = m + gate(2) * jnp.dot(oc_ref[...], pc_ref[...], preferred_element_type=F32)
    y_ref[...] = x + jnp.dot(m.astype(BF16), wo_ref[...], preferred_element_type=F32)


def _merge(x, oa, ob, oc, g1, w_bf, pa, pb, pc, wo, *, tm, name):
    T, D = x.shape
    row = lambda i: (i, 0)
    const = lambda i: (0, 0)
    n_gate = 3 * D
    gate_blk = (w_bf.shape[1] - n_gate) // n_gate
    assert gate_blk * n_gate + n_gate == w_bf.shape[1]
    return pl.pallas_call(
        _merge_kernel,
        grid=(T // tm,),
        in_specs=[pl.BlockSpec((tm, D), row),
                  pl.BlockSpec((tm, oa.shape[1]), row), pl.BlockSpec((tm, ob.shape[1]), row),
                  pl.BlockSpec((tm, oc.shape[1]), row),
                  _resident((1, D), const), _resident((D, n_gate), lambda i: (0, gate_blk)),
                  _resident(pa.shape, const), _resident(pb.shape, const),
                  _resident(pc.shape, const), _resident(wo.shape, const)],
        out_specs=pl.BlockSpec((tm, D), row),
        out_shape=jax.ShapeDtypeStruct((T, D), F32),
        compiler_params=_cparams(("arbitrary",)),
        name=name,
    )(x, oa, ob, oc, g1, w_bf, pa, pb, pc, wo)


def _conv3(cb, cw, x2, x1, x0):
    return cb + cw[0:1] * x2 + cw[1:2] * x1 + cw[2:3] * x0


def _silu(z):
    return z * _sigmoid(z)


def _shift_rows(up, prev, k):
    rolled = pltpu.roll(up, k, axis=0)
    n_prev = prev.shape[0]
    r_i = lax.broadcasted_iota(jnp.int32, (SUBLANES, up.shape[1]), 0)
    top = rolled[0:SUBLANES]
    for t in range(k):
        top = jnp.where(r_i == t, prev[n_prev - k + t:n_prev - k + t + 1], top)
    return jnp.concatenate([top, rolled[SUBLANES:]], axis=0)


def _ffn_prompt_kernel(x_ref, xh_ref, g2_ref, fu_ref, cw_ref, cb_ref, fd_ref, y_ref, tail_ref,
                       h_ref, act_ref, *, tf, tiles_per_seq, k_chunks):
    tm = x_ref.shape[0]
    dff = fd_ref.shape[0]
    first = (pl.program_id(0) % tiles_per_seq) == 0
    x = x_ref[...]
    h_ref[...] = (_rms(x) * g2_ref[...]).astype(BF16)
    hh = (_rms(xh_ref[...]) * g2_ref[...]).astype(BF16)

    def up_conv(col0):
        cols = slice(col0, col0 + tf)
        up = jnp.dot(h_ref[...], fu_ref[:, cols], preferred_element_type=F32)
        uph = jnp.dot(hh, fu_ref[:, cols], preferred_element_type=F32)
        uph = jnp.where(first, 0.0, uph)
        tail_ref[:, cols] = up[tm - SUBLANES:tm, :]
        return _conv3(cb_ref[:, cols], cw_ref[:, cols], _shift_rows(up, uph, 2), _shift_rows(up, uph, 1), up)

    y = x
    nf = dff // tf
    per_chunk = -(-nf // k_chunks)
    for j in range(nf):
        act_ref[:, j * tf:(j + 1) * tf] = (_silu(up_conv(j * tf)) * up_conv(dff + j * tf)).astype(BF16)
        if (j + 1) % per_chunk == 0 or j == nf - 1:
            lo = (j // per_chunk) * per_chunk * tf
            hi = (j + 1) * tf
            y = y + jnp.dot(act_ref[:, lo:hi], fd_ref[lo:hi, :], preferred_element_type=F32)
    y_ref[...] = y


def _ffn_prompt(x, g2, fu, cw, cb, fd, *, tm, tf, seq):
    T, D = x.shape
    dff = fd.shape[0]
    hb = tm // CONV_HALO
    row = lambda i: (i, 0)
    const = lambda i: (0, 0)
    nt = T // tm
    return pl.pallas_call(
        functools.partial(_ffn_prompt_kernel, tf=tf, tiles_per_seq=seq // tm, k_chunks=3),
        grid=(nt,),
        in_specs=[pl.BlockSpec((tm, D), row),
                  pl.BlockSpec((CONV_HALO, D), lambda i: (jnp.maximum(i * hb - 1, 0), 0)),
                  _resident((1, D), const), _resident(fu.shape, const),
                  _resident(cw.shape, const), _resident(cb.shape, const), _resident(fd.shape, const)],
        out_specs=[pl.BlockSpec((tm, D), row), pl.BlockSpec((SUBLANES, 2 * dff), row)],
        out_shape=[jax.ShapeDtypeStruct((T, D), F32), jax.ShapeDtypeStruct((nt * SUBLANES, 2 * dff), F32)],
        scratch_shapes=[pltpu.VMEM((tm, D), BF16), pltpu.VMEM((tm, dff), BF16)],
        compiler_params=_cparams(("arbitrary",)),
        name="ffn_prompt",
    )(x, x, g2, fu, cw, cb, fd)


def _ffn_sample_kernel(x_ref, g2_ref, fug_ref, fuv_ref, cwg_ref, cwv_ref, cbg_ref, cbv_ref,
                       p0g_ref, p0v_ref, p1g_ref, p1v_ref, fd_ref, y_ref, ug_ref, uv_ref, h_ref):
    j = pl.program_id(0)

    @pl.when(j == 0)
    def _():
        h_ref[...] = (_rms(x_ref[...]) * g2_ref[...]).astype(BF16)

    upg = jnp.dot(h_ref[...], fug_ref[...], preferred_element_type=F32)
    upv = jnp.dot(h_ref[...], fuv_ref[...], preferred_element_type=F32)
    ug_ref[...] = upg
    uv_ref[...] = upv
    cg = _conv3(cbg_ref[...], cwg_ref[...], p0g_ref[...], p1g_ref[...], upg)
    cv = _conv3(cbv_ref[...], cwv_ref[...], p0v_ref[...], p1v_ref[...], upv)
    d = jnp.dot((_silu(cg) * cv).astype(BF16), fd_ref[...], preferred_element_type=F32)

    @pl.when(j == 0)
    def _():
        y_ref[...] = x_ref[...] + d

    @pl.when(j > 0)
    def _():
        y_ref[...] += d


def _ffn_sample(x, g2, fu, cw, cb, pre0, pre1, fd, *, tf):
    T, D = x.shape
    dff = fd.shape[0]
    nf = dff // tf
    gcol = lambda j: (0, j)
    vcol = lambda j: (0, nf + j)
    return pl.pallas_call(
        _ffn_sample_kernel,
        grid=(nf,),
        in_specs=[pl.BlockSpec((T, D), lambda j: (0, 0)), pl.BlockSpec((1, D), lambda j: (0, 0)),
                  pl.BlockSpec((D, tf), gcol), pl.BlockSpec((D, tf), vcol),
                  pl.BlockSpec((CONV_W, tf), gcol), pl.BlockSpec((CONV_W, tf), vcol),
                  pl.BlockSpec((1, tf), gcol), pl.BlockSpec((1, tf), vcol),
                  pl.BlockSpec((T, tf), gcol), pl.BlockSpec((T, tf), vcol),
                  pl.BlockSpec((T, tf), gcol), pl.BlockSpec((T, tf), vcol),
                  pl.BlockSpec((tf, D), lambda j: (j, 0))],
        out_specs=[pl.BlockSpec((T, D), lambda j: (0, 0)),
                   pl.BlockSpec((T, tf), gcol), pl.BlockSpec((T, tf), gcol)],
        out_shape=[jax.ShapeDtypeStruct((T, D), F32),
                   jax.ShapeDtypeStruct((T, dff), F32), jax.ShapeDtypeStruct((T, dff), F32)],
        scratch_shapes=[pltpu.VMEM((T, D), BF16)],
        compiler_params=_cparams(("arbitrary",)),
        name="ffn_sample",
    )(x, g2, fu, fu, cw, cw, cb, cb, pre0, pre0, pre1, pre1, fd)


def _tiles(seq):
    tm = 512
    while seq % tm:
        tm //= 2
    return tm


def _ff_tile(dff):
    tf = 256
    while dff % tf:
        tf //= 2
    return tf


def kernel(x_prompt, x_sample, cache_k, cache_v, page_table, state_pool, state_conv, norm1_g, w_in, a_vnorm_g, a_ws, a_bs, b_qnorm_g, b_knorm_g, b_lq1, b_lk1, b_lq2, b_lk2, b_subln_g, c_w, c_scale, p_a, p_b, p_c, w_o, norm2_g, f_up, f_conv_w, f_conv_b, f_down):
    n_batch, seq, d_model = x_prompt.shape
    n_dec, dec_seq, _ = x_sample.shape
    depth, n_pool, page_size, n_heads, dh = cache_k.shape
    hd = dh // 2
    wb = n_heads * dh
    wa = a_vnorm_g.shape[-1]
    wc = c_scale.shape[-1]
    n_groups_a, chunk = a_ws.shape[1], a_ws.shape[2]
    dff = f_down.shape[1]
    n_pages = page_table.shape[1]
    past_len = n_pages * page_size
    n_state = state_pool.shape[2]
    assert dec_seq == 1 and wa == wb == wc == SEG and d_model == 2 * SEG and dh == LANES
    assert w_in.shape[2] == 6 * SEG + 3 * d_model
    assert past_len % chunk == 0 and n_state == POOL_WINDOWS[-1] - 1 and state_conv.shape[2] == CONV_W - 1
    tm = _tiles(seq)
    assert tm % chunk == 0 and tm >= POOL_HALO
    tf = _ff_tile(dff)
    pages_step = 8
    while n_pages % pages_step:
        pages_step //= 2
    heads_step = n_heads

    w_in_b, p_a_b, p_b_b, p_c_b, w_o_b, f_up_b, f_down_b = (
        a.astype(BF16) for a in (w_in, p_a, p_b, p_c, w_o, f_up, f_down))
    gi = jnp.arange(wb) // hd
    ones_bd = (gi[:, None] == gi[None, :]).astype(BF16)
    cache_k4 = cache_k.reshape(depth, n_pool, page_size * n_heads, dh)
    cache_v4 = cache_v.reshape(depth, n_pool, page_size * n_heads, dh)

    T = n_batch * seq
    y_p = x_prompt.reshape(T, d_model)
    y_s = x_sample.reshape(n_dec, d_model)
    row2 = lambda a: a.reshape(1, -1)
    outs = [[] for _ in range(9)]
    for l in range(depth):
        lam_init = 0.8 - 0.6 * math.exp(-0.3 * l)
        g1, avg, g2 = row2(norm1_g[l]), row2(a_vnorm_g[l]), row2(norm2_g[l])
        qg = row2(jnp.tile(b_qnorm_g[l], wb // hd))
        kg = row2(jnp.tile(b_knorm_g[l], wb // hd))
        lams = (row2(b_lq1[l]), row2(b_lk1[l]), row2(b_lq2[l]), row2(b_lk2[l]))
        cs, cb = row2(c_scale[l]), row2(f_conv_b[l])

        ua, q, k32, kb, v32, vt, c = _inproj(y_p, g1, w_in_b[l], avg, qg, kg, ones_bd,
                                             tm=tm, hd=hd, n_heads=n_heads, prompt=True, tk=tm)
        oa = _gating(ua, a_ws[l], a_bs[l].T, tm=tm)
        ob = _attn_prompt(q, kb, vt, *lams, b_subln_g[l].reshape(dh, 1), n_batch=n_batch, seq=seq,
                          n_heads=n_heads, hd=hd, tq=tm, heads_step=heads_step, lam_init=lam_init)
        oc = _pool_prompt(c, c_w[l], cs, seq=seq, tp=tm)
        x1 = _merge(y_p, oa, ob, oc, g1, w_in_b[l], p_a_b[l], p_b_b[l], p_c_b[l], w_o_b[l],
                    tm=tm, name="merge_prompt")
        y_p, tail = _ffn_prompt(x1, g2, f_up_b[l], f_conv_w[l], cb, f_down_b[l], tm=tm, tf=tf, seq=seq)
        tail = tail.reshape(n_batch, seq // tm, SUBLANES, 2 * dff)
        outs[0].append(k32.reshape(n_batch, seq, n_heads, dh))
        outs[1].append(v32.reshape(n_batch, seq, n_heads, dh))
        outs[5].append(c.reshape(n_batch, seq, wc)[:, seq - n_state:])
        outs[7].append(tail[:, -1, SUBLANES - (CONV_W - 1):])

        ua_s, q_s, k_s, v_s, c_s = _inproj(y_s, g1, w_in_b[l], avg, qg, kg, ones_bd,
                                           tm=n_dec, hd=hd, n_heads=n_heads, prompt=False, tk=n_dec)

        def rows16(a):
            a = a.reshape(n_dec, 1, n_heads, dh)
            a = jnp.broadcast_to(a, (n_dec, 2, n_heads, dh)).reshape(n_dec, 2 * n_heads, dh)
            return jnp.pad(a, ((0, 0), (0, 16 - 2 * n_heads), (0, 0)))

        ob_s = _attn_decode(page_table, rows16(q_s), rows16(k_s), rows16(v_s), *lams, row2(b_subln_g[l]),
                            cache_k4, cache_v4, layer=l, n_heads=n_heads, hd=hd, lam_init=lam_init,
                            n_pages_step=pages_step)
        w00 = row2(jnp.repeat(a_ws[l][:, 0, 0], wa // n_groups_a))
        b0 = row2(jnp.repeat(a_bs[l][:, 0], wa // n_groups_a))
        oa_s, oc_s = _sample_branches(ua_s, w00, b0, c_s, state_pool[l].transpose(1, 0, 2), c_w[l], cs,
                                      past_len=past_len)
        x1_s = _merge(y_s, oa_s, ob_s.reshape(n_dec, wb).astype(BF16), oc_s, g1, w_in_b[l],
                      p_a_b[l], p_b_b[l], p_c_b[l], w_o_b[l], tm=n_dec, name="merge_sample")
        y_s, ug, uv = _ffn_sample(x1_s, g2, f_up_b[l], f_conv_w[l], cb,
                                  state_conv[l][:, 0], state_conv[l][:, 1], f_down_b[l], tf=tf)
        up_s = jnp.concatenate([ug, uv], axis=-1)
        outs[2].append(k_s.reshape(n_dec, 1, n_heads, dh))
        outs[3].append(v_s.reshape(n_dec, 1, n_heads, dh))
        outs[4].append(ua_s[:, wa:].reshape(n_dec, 1, wa))
        outs[6].append(jnp.concatenate([state_pool[l][:, 1:], c_s[:, None]], axis=1))
        outs[8].append(jnp.stack([state_conv[l][:, 1], up_s], axis=1))

    st = [jnp.stack(o) for o in outs]
    return (y_p.reshape(n_batch, seq, d_model), y_s.reshape(n_dec, 1, d_model),
            st[0], st[1], st[2], st[3], st[4], st[5], st[6], st[7], st[8])
```

```python
import functools
import math

import jax
import jax.numpy as jnp
from jax import lax
from jax.experimental import pallas as pl
from jax.experimental.pallas import tpu as pltpu

F32 = jnp.float32
BF16 = jnp.bfloat16
EPS = 1e-6
NEG_INF = -1e30
LOG2E = math.log2(math.e)
POOL_WINDOWS = (2, 4, 8, 16)
POOL_HALO = 16
CONV_HALO = 16
CONV_W = 3
LANES = 128
SUBLANES = 8
SEG = 512
VMEM_LIMIT = 56 * 1024 * 1024


def _cparams(sem):
    return pltpu.CompilerParams(dimension_semantics=sem, vmem_limit_bytes=VMEM_LIMIT)


def _resident(shape, index_map):
    return pl.BlockSpec(shape, index_map, pipeline_mode=pl.Buffered(1))


def _rms(xf):
    return xf * lax.rsqrt(jnp.mean(xf * xf, axis=-1, keepdims=True) + EPS)


def _sigmoid(z):
    return 0.5 * jnp.tanh(0.5 * z) + 0.5


def _inproj_kernel(x_ref, g1_ref, w_ref, avg_ref, qg_ref, kg_ref, ones_ref, *refs, hd, n_heads, prompt, tk,
                   n_carried):
    refs = refs[n_carried:]
    if prompt:
        ua_ref, q_ref, k32_ref, kb_ref, v32_ref, vt_ref, c_ref, h_ref = refs
    else:
        ua_ref, q_ref, k32_ref, v32_ref, c_ref, h_ref = refs
    tm = x_ref.shape[0]
    dh = 2 * hd
    h_ref[...] = (_rms(x_ref[...]) * g1_ref[...]).astype(BF16)

    def seg(j):
        return jnp.dot(h_ref[...], w_ref[:, j * SEG:(j + 1) * SEG], preferred_element_type=F32)

    def group_rms(zz):
        sq = zz * zz
        hi = sq.astype(BF16)
        lo = (sq - hi.astype(F32)).astype(BF16)
        ss = (jnp.dot(hi, ones_ref[...], preferred_element_type=F32)
              + jnp.dot(lo, ones_ref[...], preferred_element_type=F32))
        return zz * lax.rsqrt(ss * (1.0 / hd) + EPS)

    def head_rows(dst_ref, val):
        for hh in range(n_heads):
            dst_ref[pl.ds(hh, tm, stride=n_heads), :] = val[:, hh * dh:(hh + 1) * dh]

    z0 = seg(0)
    z1 = seg(1)
    ua_ref[:, 0:SEG] = jax.nn.gelu(z0)
    z2 = seg(2)
    ua_ref[:, SEG:2 * SEG] = _rms(jax.nn.gelu(z1)) * avg_ref[...]
    z3 = seg(3)
    q_ref[...] = (group_rms(z2) * qg_ref[...] * (hd ** -0.5 * LOG2E)).astype(q_ref.dtype)
    zv = seg(4)
    kn = group_rms(z3) * kg_ref[...]
    zc = seg(5)
    if prompt:
        head_rows(k32_ref, kn)
        kb_ref[...] = kn.astype(BF16)
        head_rows(v32_ref, zv)
        zt = zv.T.astype(BF16)
        for t in range(vt_ref.shape[0]):
            vt_ref[t] = zt[:, t * tk:(t + 1) * tk]
    else:
        k32_ref[...] = kn
        v32_ref[...] = zv
    c_ref[...] = zc


def _inproj(x, g1, w_bf, avg, qg, kg, ones_bd, *, layer, tm, hd, n_heads, prompt, tk, carried=()):
    T, D = x.shape
    depth = w_bf.shape[0]
    nt = T // tm
    row = lambda i: (i, 0)
    const = lambda i: (0, 0)
    in_specs = [pl.BlockSpec((tm, D), row), _resident((1, D), const),
                _resident((None, D, 6 * SEG), lambda i: (layer, 0, 0)),
                _resident((1, SEG), const), _resident((1, SEG), const), _resident((1, SEG), const),
                _resident((SEG, SEG), const)]
    blk = pl.BlockSpec((tm, SEG), row)
    sd = jax.ShapeDtypeStruct
    if prompt:
        dh = SEG // n_heads
        hblk = pl.BlockSpec((tm * n_heads, dh), lambda i: (layer * nt + i, 0))
        kv_all = sd((depth * T * n_heads, dh), F32)
        out_shape = [sd((T, 2 * SEG), F32), sd((T, SEG), BF16), kv_all, sd((T, SEG), BF16),
                     kv_all, sd((T // tk, SEG, tk), BF16), sd((T, SEG), F32)]
        out_specs = [pl.BlockSpec((tm, 2 * SEG), row), blk, hblk, blk, hblk,
                     pl.BlockSpec((tm // tk, SEG, tk), lambda i: (i, 0, 0)), blk]
    else:
        out_shape = [sd((T, 2 * SEG), F32), sd((T, SEG), F32), sd((T, SEG), F32), sd((T, SEG), F32),
                     sd((T, SEG), F32)]
        out_specs = [pl.BlockSpec((tm, 2 * SEG), row), blk, blk, blk, blk]
    return pl.pallas_call(
        functools.partial(_inproj_kernel, hd=hd, n_heads=n_heads, prompt=prompt, tk=tk, n_carried=len(carried)),
        grid=(nt,), in_specs=in_specs + [pl.BlockSpec(memory_space=pl.ANY)] * len(carried),
        out_specs=out_specs, out_shape=out_shape,
        input_output_aliases=dict(zip(range(len(in_specs), len(in_specs) + len(carried)), (2, 4))),
        scratch_shapes=[pltpu.VMEM((tm, D), BF16)],
        compiler_params=_cparams(("arbitrary",)),
        name="inproj_prompt" if prompt else "inproj_sample",
    )(x, g1, w_bf, avg, qg, kg, ones_bd, *carried)


def _gating_rows(u_ref, va_ref, ws_ref, bst_ref, o_ref):
    n_groups, ch, _ = ws_ref.shape
    n_chunks = u_ref.shape[0] // ch
    cg = u_ref.shape[1] // n_groups
    r_i = lax.broadcasted_iota(jnp.int32, (ch, ch), 0)
    c_i = lax.broadcasted_iota(jnp.int32, (ch, ch), 1)
    tri = r_i >= c_i
    for g in range(n_groups):
        wm = jnp.where(tri, ws_ref[g], 0.0).astype(BF16)
        bcol = bst_ref[:, g:g + 1]
        cols = slice(g * cg, (g + 1) * cg)
        for r in range(n_chunks):
            rows = slice(r * ch, (r + 1) * ch)
            s = jnp.dot(wm, va_ref[rows, cols].astype(BF16), preferred_element_type=F32) + bcol
            o_ref[rows, cols] = (u_ref[rows, cols] * s).astype(o_ref.dtype)


def _lambda(lq1_ref, lk1_ref, lq2_ref, lk2_ref, lam_init):
    a = jnp.sum(lq1_ref[...] * lk1_ref[...], axis=-1, keepdims=True)
    b = jnp.sum(lq2_ref[...] * lk2_ref[...], axis=-1, keepdims=True)
    return jnp.exp(a) - jnp.exp(b) + lam_init


def _attn_kernel(q_ref, k_ref, vt_ref, lq1_ref, lk1_ref, lq2_ref, lk2_ref, sg_ref, o_ref,
                 acc_ref, *, tq, hd, heads_step, lam_init):
    qi = pl.program_id(2)
    dh = 2 * hd
    lane = lax.broadcasted_iota(jnp.int32, (tq, dh), 1)
    chains = []
    for hh in range(heads_step):
        q = q_ref[:, hh * dh:(hh + 1) * dh]
        zero = jnp.zeros_like(q)
        chains.append((hh, jnp.where(lane < hd, q, zero)))
        chains.append((hh, jnp.where(lane >= hd, q, zero)))
    acc_ref[...] = jnp.zeros_like(acc_ref)
    r_i = lax.broadcasted_iota(jnp.int32, (tq, tq), 0)
    c_i = lax.broadcasted_iota(jnp.int32, (tq, tq), 1)
    causal = r_i <= c_i

    def block(kj, stats, masked):
        row0 = pl.multiple_of(kj * tq, tq)

        def scores(n):
            hh, qc = chains[n]
            kblk = k_ref[pl.ds(row0, tq), hh * dh:(hh + 1) * dh]
            s = lax.dot_general(kblk, qc, (((1,), (1,)), ((), ())), preferred_element_type=F32)
            return jnp.where(causal, s, NEG_INF) if masked else s

        def weighted_values(n, p, alpha):
            hh = chains[n][0]
            vblk = vt_ref[kj, hh * dh:(hh + 1) * dh, :]
            acc_ref[n] = alpha * acc_ref[n] + jnp.dot(vblk, p, preferred_element_type=F32)

        new = []
        s_next = scores(0)
        pending = None
        for n in range(len(chains)):
            s = s_next
            if n + 1 < len(chains):
                s_next = scores(n + 1)
            m, l = stats[n]
            mn = jnp.maximum(m, jnp.max(s, axis=0, keepdims=True))
            p = jnp.exp2(s - mn)
            alpha = jnp.exp2(m - mn)
            new.append((mn, alpha * l + jnp.sum(p, axis=0, keepdims=True)))
            if pending is not None:
                weighted_values(*pending)
            pending = (n, p.astype(BF16), alpha)
        weighted_values(*pending)
        return tuple(new)

    init = tuple((jnp.full((1, tq), NEG_INF, F32), jnp.zeros((1, tq), F32)) for _ in chains)
    stats = lax.fori_loop(0, qi, lambda kj, st: block(kj, st, False), init)
    stats = block(qi, stats, True)
    lam = _lambda(lq1_ref, lk1_ref, lq2_ref, lk2_ref, lam_init)
    for hh in range(heads_step):
        o = acc_ref[2 * hh] / stats[2 * hh][1] - lam * (acc_ref[2 * hh + 1] / stats[2 * hh + 1][1])
        o = o * lax.rsqrt(jnp.mean(o * o, axis=0, keepdims=True) + EPS) * sg_ref[...]
        o_ref[:, hh * dh:(hh + 1) * dh] = (o * (1.0 - lam_init)).T.astype(o_ref.dtype)


def _attn_prompt(q, kb, vt, lq1, lk1, lq2, lk2, sg_col, *, n_batch, seq, n_heads, hd, tq, heads_step,
                 lam_init):
    T, wb = q.shape
    dh = 2 * hd
    nq = seq // tq
    wblk = heads_step * dh
    small = lambda b, h, i: (0, 0)
    return pl.pallas_call(
        functools.partial(_attn_kernel, tq=tq, hd=hd, heads_step=heads_step, lam_init=lam_init),
        grid=(n_batch, n_heads // heads_step, nq),
        in_specs=[pl.BlockSpec((tq, wblk), lambda b, h, i: (b * nq + i, h)),
                  pl.BlockSpec((seq, wblk), lambda b, h, i: (b, h)),
                  pl.BlockSpec((nq, wblk, tq), lambda b, h, i: (b, h, 0)),
                  pl.BlockSpec((1, hd), small), pl.BlockSpec((1, hd), small),
                  pl.BlockSpec((1, hd), small), pl.BlockSpec((1, hd), small),
                  pl.BlockSpec((dh, 1), small)],
        out_specs=pl.BlockSpec((tq, wblk), lambda b, h, i: (b * nq + i, h)),
        out_shape=jax.ShapeDtypeStruct((T, wb), BF16),
        scratch_shapes=[pltpu.VMEM((2 * heads_step, dh, tq), F32)],
        compiler_params=_cparams(("arbitrary", "arbitrary", "arbitrary")),
        name="attn_prompt",
    )(q, kb, vt, lq1, lk1, lq2, lk2, sg_col)


def _pool_rows(c_ref, halo_ref, cw_ref, cs_ref, o_ref, ext_ref, *, tiles_per_seq):
    tp = c_ref.shape[0]
    cg = c_ref.shape[1] // cw_ref.shape[0]
    t_in_seq = pl.program_id(0) % tiles_per_seq
    ext_ref[0:POOL_HALO, :] = jnp.where(t_in_seq == 0, 0.0, halo_ref[...])
    ext_ref[POOL_HALO:, :] = c_ref[...]
    pos = t_in_seq * tp + lax.broadcasted_iota(jnp.int32, (tp, 1), 0)
    for g, win in enumerate(POOL_WINDOWS):
        cols = slice(g * cg, (g + 1) * cg)
        wsum = ext_ref[POOL_HALO:POOL_HALO + tp, cols]
        for jj in range(1, win):
            wsum = wsum + ext_ref[POOL_HALO - jj:POOL_HALO - jj + tp, cols]
        cnt = jnp.minimum(pos + 1, win).astype(F32)
        pooled = wsum / cnt - c_ref[:, cols]
        out = jnp.dot(pooled.astype(BF16), cw_ref[g].astype(BF16), preferred_element_type=F32)
        o_ref[:, cols] = (out * cs_ref[:, cols]).astype(o_ref.dtype)


def _sample_branches_kernel(u_ref, va_ref, w00_ref, b0_ref, c_ref, st_ref, cw_ref, cs_ref,
                            oa_ref, oc_ref, *, cg, past_len):
    oa_ref[...] = (u_ref[...] * (w00_ref[...] * va_ref[...] + b0_ref[...])).astype(oa_ref.dtype)
    n_state = st_ref.shape[0]
    for g, win in enumerate(POOL_WINDOWS):
        cols = slice(g * cg, (g + 1) * cg)
        wsum = c_ref[:, cols]
        for jj in range(1, win):
            wsum = wsum + st_ref[n_state - jj][:, cols]
        pooled = wsum / float(min(past_len + 1, win)) - c_ref[:, cols]
        out = jnp.dot(pooled.astype(BF16), cw_ref[g].astype(BF16), preferred_element_type=F32)
        oc_ref[:, cols] = (out * cs_ref[:, cols]).astype(oc_ref.dtype)


def _sample_branches(ua, w00, b0, c, st_t, cw, cs, *, past_len):
    T, wc = c.shape
    wa = ua.shape[1] // 2
    full2 = lambda a: pl.BlockSpec(a.shape, lambda i: (0,) * a.ndim)
    return pl.pallas_call(
        functools.partial(_sample_branches_kernel, cg=wc // cw.shape[0], past_len=past_len),
        grid=(1,),
        in_specs=[pl.BlockSpec((T, wa), lambda i: (0, 0)), pl.BlockSpec((T, wa), lambda i: (0, 1)),
                  full2(w00), full2(b0), full2(c), full2(st_t), full2(cw), full2(cs)],
        out_specs=[pl.BlockSpec((T, wa), lambda i: (0, 0)), pl.BlockSpec((T, wc), lambda i: (0, 0))],
        out_shape=[jax.ShapeDtypeStruct((T, wa), BF16), jax.ShapeDtypeStruct((T, wc), BF16)],
        compiler_params=_cparams(("arbitrary",)),
        name="branches_sample",
    )(ua, ua, w00, b0, c, st_t, cw, cs)


def _decode_attn_kernel(pt_ref, q_ref, kn_ref, vn_ref, lq1_ref, lk1_ref, lq2_ref, lk2_ref, sg_ref,
                        *refs, n_pages_step, n_heads, hd, lam_init):
    del pt_ref
    kp = refs[:n_pages_step]
    vp = refs[n_pages_step:2 * n_pages_step]
    o_ref, m_ref, l_ref, acc_ref = refs[2 * n_pages_step:]
    s_idx = pl.program_id(1)
    rows, dh = q_ref.shape
    r_i = lax.broadcasted_iota(jnp.int32, (rows, dh), 0)
    l_i = lax.broadcasted_iota(jnp.int32, (rows, dh), 1)
    comp_sel = (l_i >= hd) == ((r_i // n_heads) % 2 == 1)
    qm = jnp.where(comp_sel, q_ref[...], 0.0)
    qm_b = qm.astype(BF16)

    @pl.when(s_idx == 0)
    def _():
        m_ref[...] = jnp.full_like(m_ref, NEG_INF)
        l_ref[...] = jnp.zeros_like(l_ref)
        acc_ref[...] = jnp.zeros_like(acc_ref)

    page_rows = kp[0].shape[0]
    pr_i = lax.broadcasted_iota(jnp.int32, (rows, page_rows), 0)
    pl_i = lax.broadcasted_iota(jnp.int32, (rows, page_rows), 1)
    head_ok = (pl_i % n_heads) == (pr_i % n_heads)
    scores = []
    for r in range(n_pages_step):
        s = lax.dot_general(qm_b, kp[r][...].astype(BF16), (((1,), (1,)), ((), ())),
                            preferred_element_type=F32)
        scores.append(jnp.where(head_ok, s, NEG_INF))
    m_old = m_ref[...]
    m_new = m_old
    for s in scores:
        m_new = jnp.maximum(m_new, jnp.max(s, axis=-1, keepdims=True))
    alpha = jnp.exp2(m_old - m_new)
    l_new = alpha * l_ref[...]
    acc = alpha * acc_ref[...]
    for r in range(n_pages_step):
        p = jnp.exp2(scores[r] - m_new)
        l_new = l_new + jnp.sum(p, axis=-1, keepdims=True)
        acc = acc + jnp.dot(p.astype(BF16), vp[r][...].astype(BF16), preferred_element_type=F32)
    m_ref[...] = m_new
    l_ref[...] = l_new
    acc_ref[...] = acc

    @pl.when(s_idx == pl.num_programs(1) - 1)
    def _():
        s_self = jnp.sum(qm * kn_ref[...], axis=-1, keepdims=True)
        m_fin = jnp.maximum(m_new, s_self)
        a2 = jnp.exp2(m_new - m_fin)
        p_self = jnp.exp2(s_self - m_fin)
        l_fin = a2 * l_new + p_self
        o_all = (a2 * acc + p_self * vn_ref[...]) / l_fin
        lam = _lambda(lq1_ref, lk1_ref, lq2_ref, lk2_ref, lam_init)
        o = o_all[0:n_heads] - lam * o_all[n_heads:2 * n_heads]
        o_ref[...] = _rms(o) * sg_ref[...] * (1.0 - lam_init)


def _attn_decode(page_table, q16, kn16, vn16, lq1, lk1, lq2, lk2, sg_row, cache_k4, cache_v4, *,
                 layer, n_heads, hd, lam_init, n_pages_step):
    nb, rows, dh = q16.shape
    n_pages = page_table.shape[1]
    page_rows = cache_k4.shape[2]
    gp = n_pages_step
    small = lambda b, s, pt: (0, 0)
    per_b = lambda b, s, pt: (b, 0, 0)

    def page_spec(r):
        return pl.BlockSpec((None, None, page_rows, dh), lambda b, s, pt: (layer, pt[b, s * gp + r], 0, 0))

    in_specs = ([pl.BlockSpec((None, rows, dh), per_b)] * 3
                + [pl.BlockSpec((1, hd), small)] * 4 + [pl.BlockSpec((1, dh), small)]
                + [page_spec(r) for r in range(gp)] * 2)
    grid_spec = pltpu.PrefetchScalarGridSpec(
        num_scalar_prefetch=1, grid=(nb, n_pages // gp), in_specs=in_specs,
        out_specs=pl.BlockSpec((None, n_heads, dh), per_b),
        scratch_shapes=[pltpu.VMEM((rows, 1), F32), pltpu.VMEM((rows, 1), F32), pltpu.VMEM((rows, dh), F32)])
    return pl.pallas_call(
        functools.partial(_decode_attn_kernel, n_pages_step=gp, n_heads=n_heads, hd=hd, lam_init=lam_init),
        grid_spec=grid_spec,
        out_shape=jax.ShapeDtypeStruct((nb, n_heads, dh), F32),
        compiler_params=_cparams(("arbitrary", "arbitrary")),
        name="attn_decode",
    )(page_table, q16, kn16, vn16, lq1, lk1, lq2, lk2, sg_row, *([cache_k4] * gp), *([cache_v4] * gp))


def _merge_prompt_kernel(x_ref, u_ref, va_ref, c_ref, halo_ref, ob_ref, ws_ref, bst_ref, cw_ref, cs_ref,
                         g1_ref, wg_ref, pa_ref, pb_ref, pc_ref, wo_ref, y_ref, oa_ref, oc_ref, ext_ref, *,
                         tiles_per_seq):
    def branch_a():
        _gating_rows(u_ref, va_ref, ws_ref, bst_ref, oa_ref)

    def branch_c():
        _pool_rows(c_ref, halo_ref, cw_ref, cs_ref, oc_ref, ext_ref, tiles_per_seq=tiles_per_seq)

    _merge_kernel(x_ref, oa_ref, ob_ref, oc_ref, g1_ref, wg_ref, pa_ref, pb_ref, pc_ref, wo_ref, y_ref,
                  branch_a=branch_a, branch_c=branch_c)


def _merge_kernel(x_ref, oa_ref, ob_ref, oc_ref, g1_ref, wg_ref, pa_ref, pb_ref, pc_ref, wo_ref, y_ref, *,
                  branch_a=None, branch_c=None):
    x = x_ref[...]
    D = x.shape[1]
    h = (_rms(x) * g1_ref[...]).astype(BF16)

    def gate(j):
        return _sigmoid(jnp.dot(h, wg_ref[:, j * D:(j + 1) * D], preferred_element_type=F32))

    ga = gate(0)
    if branch_a is not None:
        branch_a()
    gb = gate(1)
    if branch_c is not None:
        branch_c()
    gc = gate(2)
    m = ga * jnp.dot(oa_ref[...], pa_ref[...], preferred_element_type=F32)
    m = m + gb * jnp.dot(ob_ref[...], pb_ref[...], preferred_element_type=F32)
    m = m + gc * jnp.dot(oc_ref[...], pc_ref[...], preferred_element_type=F32)
    y_ref[...] = x + jnp.dot(m.astype(BF16), wo_ref[...], preferred_element_type=F32)


def _merge_weight_specs(D, w_bf, pa, pb, pc, wo, layer):
    n_gate = 3 * D
    gate_blk = (w_bf.shape[2] - n_gate) // n_gate
    assert gate_blk * n_gate + n_gate == w_bf.shape[2]
    wspec = lambda a: _resident((None,) + a.shape[1:], lambda i: (layer, 0, 0))
    return [_resident((1, D), lambda i: (0, 0)), _resident((None, D, n_gate), lambda i: (layer, 0, gate_blk)),
            wspec(pa), wspec(pb), wspec(pc), wspec(wo)]


def _merge_sample(x, oa, ob, oc, g1, w_bf, pa, pb, pc, wo, *, layer):
    T, D = x.shape
    full = lambda a: pl.BlockSpec(a.shape, lambda i: (0, 0))
    return pl.pallas_call(
        _merge_kernel,
        grid=(1,),
        in_specs=[full(x), full(oa), full(ob), full(oc)] + _merge_weight_specs(D, w_bf, pa, pb, pc, wo, layer),
        out_specs=full(x),
        out_shape=jax.ShapeDtypeStruct((T, D), F32),
        compiler_params=_cparams(("arbitrary",)),
        name="merge_sample",
    )(x, oa, ob, oc, g1, w_bf, pa, pb, pc, wo)


def _merge_prompt(x, ua, c, ob, ws, bst, cw, cs, g1, w_bf, pa, pb, pc, wo, *, layer, tm, seq):
    T, D = x.shape
    wa = ua.shape[1] // 2
    wc = c.shape[1]
    hb = tm // POOL_HALO
    row = lambda i: (i, 0)
    small = lambda a: _resident(a.shape, lambda i: (0,) * a.ndim)
    return pl.pallas_call(
        functools.partial(_merge_prompt_kernel, tiles_per_seq=seq // tm),
        grid=(T // tm,),
        in_specs=[pl.BlockSpec((tm, D), row),
                  pl.BlockSpec((tm, wa), row), pl.BlockSpec((tm, wa), lambda i: (i, 1)),
                  pl.BlockSpec((tm, wc), row),
                  pl.BlockSpec((POOL_HALO, wc), lambda i: (jnp.maximum(i * hb - 1, 0), 0)),
                  pl.BlockSpec((tm, ob.shape[1]), row),
                  small(ws), small(bst), small(cw), small(cs)] + _merge_weight_specs(D, w_bf, pa, pb, pc, wo, layer),
        out_specs=pl.BlockSpec((tm, D), row),
        out_shape=jax.ShapeDtypeStruct((T, D), F32),
        scratch_shapes=[pltpu.VMEM((tm, wa), BF16), pltpu.VMEM((tm, wc), BF16),
                        pltpu.VMEM((tm + POOL_HALO, wc), F32)],
        compiler_params=_cparams(("arbitrary",)),
        name="merge_prompt",
    )(x, ua, ua, c, c, ob, ws, bst, cw, cs, g1, w_bf, pa, pb, pc, wo)


def _conv3(cb, cw, x2, x1, x0):
    return cb + cw[0:1] * x2 + cw[1:2] * x1 + cw[2:3] * x0


def _silu(z):
    return z * _sigmoid(z)


def _shift_rows(up, prev, k):
    rolled = pltpu.roll(up, k, axis=0)
    n_prev = prev.shape[0]
    r_i = lax.broadcasted_iota(jnp.int32, (SUBLANES, up.shape[1]), 0)
    top = rolled[0:SUBLANES]
    for t in range(k):
        top = jnp.where(r_i == t, prev[n_prev - k + t:n_prev - k + t + 1], top)
    return jnp.concatenate([top, rolled[SUBLANES:]], axis=0)


def _ffn_prompt_kernel(x_ref, xh_ref, g2_ref, fu_ref, cw_ref, cb_ref, fd_ref, y_ref, tail_ref,
                       h_ref, act_ref, *, tf, tiles_per_seq, k_chunks):
    tm = x_ref.shape[0]
    dff = fd_ref.shape[0]
    first = (pl.program_id(0) % tiles_per_seq) == 0
    x = x_ref[...]
    h_ref[...] = (_rms(x) * g2_ref[...]).astype(BF16)
    hh = (_rms(xh_ref[...]) * g2_ref[...]).astype(BF16)

    def up_proj(col0):
        cols = slice(col0, col0 + tf)
        up = jnp.dot(h_ref[...], fu_ref[:, cols], preferred_element_type=F32)
        uph = jnp.dot(hh, fu_ref[:, cols], preferred_element_type=F32)
        return up, uph

    def conv(col0, up, uph):
        cols = slice(col0, col0 + tf)
        uph = jnp.where(first, 0.0, uph)
        tail_ref[:, cols] = up[tm - SUBLANES:tm, :]
        return _conv3(cb_ref[:, cols], cw_ref[:, cols], _shift_rows(up, uph, 2), _shift_rows(up, uph, 1), up)

    y = x
    nf = dff // tf
    per_chunk = -(-nf // k_chunks)
    nxt = (up_proj(0), up_proj(dff))
    for j in range(nf):
        (ug, ugh), (uv, uvh) = nxt
        if j + 1 < nf:
            nxt = (up_proj((j + 1) * tf), up_proj(dff + (j + 1) * tf))
        act = _silu(conv(j * tf, ug, ugh)) * conv(dff + j * tf, uv, uvh)
        act_ref[:, j * tf:(j + 1) * tf] = act.astype(BF16)
        if (j + 1) % per_chunk == 0 or j == nf - 1:
            lo = (j // per_chunk) * per_chunk * tf
            hi = (j + 1) * tf
            y = y + jnp.dot(act_ref[:, lo:hi], fd_ref[lo:hi, :], preferred_element_type=F32)
    y_ref[...] = y


def _ffn_prompt(x, g2, fu, cw, cb, fd, *, layer, tm, tf, seq):
    T, D = x.shape
    dff = fd.shape[1]
    lsel = lambda i: (layer, 0, 0)
    hb = tm // CONV_HALO
    row = lambda i: (i, 0)
    const = lambda i: (0, 0)
    nt = T // tm
    return pl.pallas_call(
        functools.partial(_ffn_prompt_kernel, tf=tf, tiles_per_seq=seq // tm, k_chunks=3),
        grid=(nt,),
        in_specs=[pl.BlockSpec((tm, D), row),
                  pl.BlockSpec((CONV_HALO, D), lambda i: (jnp.maximum(i * hb - 1, 0), 0)),
                  _resident((1, D), const), _resident((None,) + fu.shape[1:], lsel),
                  _resident(cw.shape, const), _resident(cb.shape, const),
                  _resident((None,) + fd.shape[1:], lsel)],
        out_specs=[pl.BlockSpec((tm, D), row), pl.BlockSpec((SUBLANES, 2 * dff), row)],
        out_shape=[jax.ShapeDtypeStruct((T, D), F32), jax.ShapeDtypeStruct((nt * SUBLANES, 2 * dff), F32)],
        scratch_shapes=[pltpu.VMEM((tm, D), BF16), pltpu.VMEM((tm, dff), BF16)],
        compiler_params=_cparams(("arbitrary",)),
        name="ffn_prompt",
    )(x, x, g2, fu, cw, cb, fd)


def _ffn_sample_kernel(x_ref, g2_ref, fug_ref, fuv_ref, cwg_ref, cwv_ref, cbg_ref, cbv_ref,
                       p0g_ref, p0v_ref, p1g_ref, p1v_ref, fd_ref, y_ref, ug_ref, uv_ref, h_ref):
    j = pl.program_id(0)

    @pl.when(j == 0)
    def _():
        h_ref[...] = (_rms(x_ref[...]) * g2_ref[...]).astype(BF16)

    upg = jnp.dot(h_ref[...], fug_ref[...], preferred_element_type=F32)
    upv = jnp.dot(h_ref[...], fuv_ref[...], preferred_element_type=F32)
    ug_ref[...] = upg
    uv_ref[...] = upv
    cg = _conv3(cbg_ref[...], cwg_ref[...], p0g_ref[...], p1g_ref[...], upg)
    cv = _conv3(cbv_ref[...], cwv_ref[...], p0v_ref[...], p1v_ref[...], upv)
    d = jnp.dot((_silu(cg) * cv).astype(BF16), fd_ref[...], preferred_element_type=F32)

    @pl.when(j == 0)
    def _():
        y_ref[...] = x_ref[...] + d

    @pl.when(j > 0)
    def _():
        y_ref[...] += d


def _ffn_sample(x, g2, fu, cw, cb, pre0, pre1, fd, *, layer, tf):
    T, D = x.shape
    dff = fd.shape[1]
    nf = dff // tf
    gcol = lambda j: (0, j)
    vcol = lambda j: (0, nf + j)
    return pl.pallas_call(
        _ffn_sample_kernel,
        grid=(nf,),
        in_specs=[pl.BlockSpec((T, D), lambda j: (0, 0)), pl.BlockSpec((1, D), lambda j: (0, 0)),
                  pl.BlockSpec((None, D, tf), lambda j: (layer, 0, j)),
                  pl.BlockSpec((None, D, tf), lambda j: (layer, 0, nf + j)),
                  pl.BlockSpec((CONV_W, tf), gcol), pl.BlockSpec((CONV_W, tf), vcol),
                  pl.BlockSpec((1, tf), gcol), pl.BlockSpec((1, tf), vcol),
                  pl.BlockSpec((T, tf), gcol), pl.BlockSpec((T, tf), vcol),
                  pl.BlockSpec((T, tf), gcol), pl.BlockSpec((T, tf), vcol),
                  pl.BlockSpec((None, tf, D), lambda j: (layer, j, 0))],
        out_specs=[pl.BlockSpec((T, D), lambda j: (0, 0)),
                   pl.BlockSpec((T, tf), gcol), pl.BlockSpec((T, tf), gcol)],
        out_shape=[jax.ShapeDtypeStruct((T, D), F32),
                   jax.ShapeDtypeStruct((T, dff), F32), jax.ShapeDtypeStruct((T, dff), F32)],
        scratch_shapes=[pltpu.VMEM((T, D), BF16)],
        compiler_params=_cparams(("arbitrary",)),
        name="ffn_sample",
    )(x, g2, fu, fu, cw, cw, cb, cb, pre0, pre0, pre1, pre1, fd)


def _tiles(seq, largest):
    tm = largest
    while seq % tm:
        tm //= 2
    return tm


def _ff_tile(dff):
    tf = 256
    while dff % tf:
        tf //= 2
    return tf


def kernel(x_prompt, x_sample, cache_k, cache_v, page_table, state_pool, state_conv, norm1_g, w_in, a_vnorm_g, a_ws, a_bs, b_qnorm_g, b_knorm_g, b_lq1, b_lk1, b_lq2, b_lk2, b_subln_g, c_w, c_scale, p_a, p_b, p_c, w_o, norm2_g, f_up, f_conv_w, f_conv_b, f_down):
    n_batch, seq, d_model = x_prompt.shape
    n_dec, dec_seq, _ = x_sample.shape
    depth, n_pool, page_size, n_heads, dh = cache_k.shape
    hd = dh // 2
    wb = n_heads * dh
    wa = a_vnorm_g.shape[-1]
    wc = c_scale.shape[-1]
    n_groups_a, chunk = a_ws.shape[1], a_ws.shape[2]
    dff = f_down.shape[1]
    n_pages = page_table.shape[1]
    past_len = n_pages * page_size
    n_state = state_pool.shape[2]
    assert dec_seq == 1 and wa == wb == wc == SEG and d_model == 2 * SEG and dh == LANES
    assert w_in.shape[2] == 6 * SEG + 3 * d_model
    assert past_len % chunk == 0 and n_state == POOL_WINDOWS[-1] - 1 and state_conv.shape[2] == CONV_W - 1
    tm = _tiles(seq, 512)
    tmm = _tiles(seq, 1024)
    assert tm % chunk == 0 and tm >= POOL_HALO and tmm % tm == 0
    tf = _ff_tile(dff)
    pages_step = 16
    while n_pages % pages_step:
        pages_step //= 2
    heads_step = n_heads

    w_in_b, p_a_b, p_b_b, p_c_b, w_o_b, f_up_b, f_down_b = (
        a.astype(BF16) for a in (w_in, p_a, p_b, p_c, w_o, f_up, f_down))
    gi = jnp.arange(wb) // hd
    ones_bd = (gi[:, None] == gi[None, :]).astype(BF16)
    cache_k4 = cache_k.reshape(depth, n_pool, page_size * n_heads, dh)
    cache_v4 = cache_v.reshape(depth, n_pool, page_size * n_heads, dh)

    T = n_batch * seq
    y_p = x_prompt.reshape(T, d_model)
    y_s = x_sample.reshape(n_dec, d_model)
    row2 = lambda a: a.reshape(1, -1)
    outs = [[] for _ in range(9)]
    for l in range(depth):
        lam_init = 0.8 - 0.6 * math.exp(-0.3 * l)
        g1, avg, g2 = row2(norm1_g[l]), row2(a_vnorm_g[l]), row2(norm2_g[l])
        qg = row2(jnp.tile(b_qnorm_g[l], wb // hd))
        kg = row2(jnp.tile(b_knorm_g[l], wb // hd))
        lams = (row2(b_lq1[l]), row2(b_lk1[l]), row2(b_lq2[l]), row2(b_lk2[l]))
        cs, cb = row2(c_scale[l]), row2(f_conv_b[l])

        ua, q, k_all, kb, v_all, vt, c = _inproj(y_p, g1, w_in_b, avg, qg, kg, ones_bd, layer=l,
                                                 tm=tmm, hd=hd, n_heads=n_heads, prompt=True, tk=tm,
                                                 carried=(k_all, v_all) if l else ())
        ob = _attn_prompt(q, kb, vt, *lams, b_subln_g[l].reshape(dh, 1), n_batch=n_batch, seq=seq,
                          n_heads=n_heads, hd=hd, tq=tm, heads_step=heads_step, lam_init=lam_init)
        x1 = _merge_prompt(y_p, ua, c, ob, a_ws[l], a_bs[l].T, c_w[l], cs, g1, w_in_b, p_a_b, p_b_b, p_c_b, w_o_b,
                           layer=l, tm=tm, seq=seq)
        y_p, tail = _ffn_prompt(x1, g2, f_up_b, f_conv_w[l], cb, f_down_b, layer=l, tm=tmm, tf=tf, seq=seq)
        tail = tail.reshape(n_batch, seq // tmm, SUBLANES, 2 * dff)
        outs[5].append(c.reshape(n_batch, seq, wc)[:, seq - n_state:])
        outs[7].append(tail[:, -1, SUBLANES - (CONV_W - 1):])

        ua_s, q_s, k_s, v_s, c_s = _inproj(y_s, g1, w_in_b, avg, qg, kg, ones_bd, layer=l,
                                           tm=n_dec, hd=hd, n_heads=n_heads, prompt=False, tk=n_dec)

        def rows16(a):
            a = a.reshape(n_dec, 1, n_heads, dh)
            a = jnp.broadcast_to(a, (n_dec, 2, n_heads, dh)).reshape(n_dec, 2 * n_heads, dh)
            return jnp.pad(a, ((0, 0), (0, 16 - 2 * n_heads), (0, 0)))

        ob_s = _attn_decode(page_table, rows16(q_s), rows16(k_s), rows16(v_s), *lams, row2(b_subln_g[l]),
                            cache_k4, cache_v4, layer=l, n_heads=n_heads, hd=hd, lam_init=lam_init,
                            n_pages_step=pages_step)
        w00 = row2(jnp.repeat(a_ws[l][:, 0, 0], wa // n_groups_a))
        b0 = row2(jnp.repeat(a_bs[l][:, 0], wa // n_groups_a))
        oa_s, oc_s = _sample_branches(ua_s, w00, b0, c_s, state_pool[l].transpose(1, 0, 2), c_w[l], cs,
                                      past_len=past_len)
        x1_s = _merge_sample(y_s, oa_s, ob_s.reshape(n_dec, wb).astype(BF16), oc_s, g1, w_in_b,
                             p_a_b, p_b_b, p_c_b, w_o_b, layer=l)
        y_s, ug, uv = _ffn_sample(x1_s, g2, f_up_b, f_conv_w[l], cb,
                                  state_conv[l][:, 0], state_conv[l][:, 1], f_down_b, layer=l, tf=tf)
        up_s = jnp.concatenate([ug, uv], axis=-1)
        outs[2].append(k_s.reshape(n_dec, 1, n_heads, dh))
        outs[3].append(v_s.reshape(n_dec, 1, n_heads, dh))
        outs[4].append(ua_s[:, wa:].reshape(n_dec, 1, wa))
        outs[6].append(jnp.concatenate([state_pool[l][:, 1:], c_s[:, None]], axis=1))
        outs[8].append(jnp.stack([state_conv[l][:, 1], up_s], axis=1))

    st = [jnp.stack(o) if o else None for o in outs]
    kv_shape = (depth, n_batch, seq, n_heads, dh)
    return (y_p.reshape(n_batch, seq, d_model), y_s.reshape(n_dec, 1, d_model),
            k_all.reshape(kv_shape), v_all.reshape(kv_shape), st[2], st[3], st[4], st[5], st[6], st[7], st[8])
```
